```python
import jax
import jax.numpy as jnp
from jax import lax
import numpy as np


D_MODEL = 1024
BATCH = 16
SEQ = 256
DEPTH = 2
DEC_BATCH = 4
DEC_SEQ = 2048
PAST_LEN = 512

GRID_W = 64
H_A = 8
KV_A = 2
HD_A = 64
H_B = 8
NOPE_B = 64
ROPE_B = 32
VD_B = 64
KV_RANK = 256
D_RNN = 1024
RNN_BLOCKS = 8
RNN_BS = D_RNN // RNN_BLOCKS
RNN_CONV = 4
RNN_PAD = (2, 1)
RG_C = 8.0
D_FF = 2816
FFN_CONV = 3
FFN_PAD = (1, 1)
N_BRANCH = 3
Q_BLOCK = 128
ROPE_THETA = 10000.0
EPS = 1e-6
QA_W = H_A * HD_A
KA_W = KV_A * HD_A
QB_W = H_B * (NOPE_B + ROPE_B)
IN_SECTIONS = (QA_W, KA_W, KA_W, QB_W, KV_RANK, ROPE_B, D_RNN, D_RNN, N_BRANCH * D_MODEL)
IN_SPLITS = tuple(sum(IN_SECTIONS[:i + 1]) for i in range(len(IN_SECTIONS) - 1))
IN_WIDTH = sum(IN_SECTIONS)

kernel_name = 'hybrid_flow_gqa_mla_rglru_convffn_step'


def rmsnorm(x, g):
    xf = x.astype(jnp.float32)
    y = xf * lax.rsqrt(jnp.mean(xf * xf, axis=-1, keepdims=True) + EPS) * g.astype(jnp.float32)
    return y.astype(x.dtype)


def axial_angles(rows, dim):
    row = jnp.repeat(jnp.arange(rows, dtype=jnp.float32), GRID_W)
    col = jnp.tile(jnp.arange(GRID_W, dtype=jnp.float32), rows)
    n = dim // 4
    inv = ROPE_THETA ** (-jnp.arange(n, dtype=jnp.float32) / n)
    ar = row[:, None] * inv
    ac = col[:, None] * inv
    return (jnp.cos(ar), jnp.sin(ar), jnp.cos(ac), jnp.sin(ac))


def rotate(x, cos, sin):
    x1, x2 = jnp.split(x, 2, axis=-1)
    c = cos[:, None, :]
    s = sin[:, None, :]
    return jnp.concatenate([x1 * c - x2 * s, x1 * s + x2 * c], axis=-1)


def axial_rope(x, ang):
    cr, sr, cc, sc = ang
    xr, xc = jnp.split(x.astype(jnp.float32), 2, axis=-1)
    return jnp.concatenate([rotate(xr, cr, sr), rotate(xc, cc, sc)], axis=-1).astype(x.dtype)


def block_attend(q, k, v, scale):
    B, T, KV, G, DH = q.shape
    DV = v.shape[-1]
    nb = T // Q_BLOCK
    qb = jnp.moveaxis(q.reshape(B, nb, Q_BLOCK, KV, G, DH), 1, 0)

    def one(qblk):
        s = jnp.einsum('bqkgd,bskd->bkgqs', qblk, k, preferred_element_type=jnp.float32) * scale
        p = jax.nn.softmax(s, axis=-1).astype(v.dtype)
        return jnp.einsum('bkgqs,bskd->bqkgd', p, v)

    o = lax.map(one, qb)
    return jnp.moveaxis(o, 0, 1).reshape(B, T, KV, G, DV)


def dwconv(x, w, b, pad):
    y = lax.conv_general_dilated(x, w[:, None, :].astype(x.dtype), window_strides=(1,), padding=(pad,),
                                 dimension_numbers=('NWC', 'WIO', 'NWC'), feature_group_count=x.shape[-1])
    return y + b.astype(x.dtype)


def rglru(u, w_r, b_r, w_i, b_i, lam, h0, reverse):
    B, T, _ = u.shape
    uf = u.astype(jnp.float32)
    ub = uf.reshape(B, T, RNN_BLOCKS, RNN_BS)
    r = jax.nn.sigmoid(jnp.einsum('btnc,ncd->btnd', ub, w_r.astype(jnp.float32)).reshape(B, T, D_RNN) + b_r.astype(jnp.float32))
    i = jax.nn.sigmoid(jnp.einsum('btnc,ncd->btnd', ub, w_i.astype(jnp.float32)).reshape(B, T, D_RNN) + b_i.astype(jnp.float32))
    log_a = RG_C * r * jax.nn.log_sigmoid(lam.astype(jnp.float32))
    a = jnp.exp(log_a)
    bx = jnp.sqrt(-jnp.expm1(2.0 * log_a)) * (i * uf)

    def combine(e1, e2):
        a1, b1 = e1
        a2, b2 = e2
        return a1 * a2, a2 * b1 + b2

    A, Hs = lax.associative_scan(combine, (a, bx), axis=1, reverse=reverse)
    return Hs + A * h0.astype(jnp.float32)[:, None, :]


def mixer(h, lp, angs, cache):
    B, T, _ = h.shape
    qa, ka, va, qb, ckv, krope, xr, yr, gl = jnp.split(h @ lp['w_in'], IN_SPLITS, axis=-1)
    qa = rmsnorm(qa.reshape(B, T, H_A, HD_A), lp['g_qa'])
    ka = rmsnorm(ka.reshape(B, T, KV_A, HD_A), lp['g_ka'])
    va = va.reshape(B, T, KV_A, HD_A)
    qb = qb.reshape(B, T, H_B, NOPE_B + ROPE_B)
    ckv = rmsnorm(ckv, lp['g_ckv'])
    kr = krope[:, :, None, :]
    if cache is None:
        ka_all, va_all, ckv_all, kr_all = ka, va, ckv, kr
        h0f = jnp.zeros((B, D_RNN), jnp.float32)
        h0b = h0f
    else:
        ang_a, ang_b = angs
        qa = axial_rope(qa, ang_a)
        ka = axial_rope(ka, ang_a)
        qb = jnp.concatenate([qb[..., :NOPE_B], axial_rope(qb[..., NOPE_B:], ang_b)], axis=-1)
        kr = axial_rope(kr, ang_b)
        ck, cv, cc, ckr, h0f, h0b = cache
        ka_all = jnp.concatenate([ck.astype(ka.dtype), ka], axis=1)
        va_all = jnp.concatenate([cv.astype(va.dtype), va], axis=1)
        ckv_all = jnp.concatenate([cc.astype(ckv.dtype), ckv], axis=1)
        kr_all = jnp.concatenate([ckr[:, :, None, :].astype(kr.dtype), kr], axis=1)
    S = ka_all.shape[1]
    o_a = block_attend(qa.reshape(B, T, KV_A, H_A // KV_A, HD_A), ka_all, va_all, HD_A ** -0.5).reshape(B, T, QA_W)
    kb = jnp.concatenate([(ckv_all @ lp['w_uk']).reshape(B, S, H_B, NOPE_B),
                          jnp.broadcast_to(kr_all, (B, S, H_B, ROPE_B))], axis=-1)
    vb = (ckv_all @ lp['w_uv']).reshape(B, S, H_B, VD_B)
    o_b = block_attend(qb[:, :, :, None, :], kb, vb, (NOPE_B + ROPE_B) ** -0.5).reshape(B, T, H_B * VD_B)
    u = dwconv(xr, lp['conv_rnn_w'], lp['conv_rnn_b'], RNN_PAD)
    hf = rglru(u, lp['w_rg'][0], lp['b_rg'][0], lp['w_ig'][0], lp['b_ig'][0], lp['lam'][0], h0f, False)
    hb = rglru(u, lp['w_rg'][1], lp['b_rg'][1], lp['w_ig'][1], lp['b_ig'][1], lp['lam'][1], h0b, True)
    o_c = jax.nn.gelu(yr) * (hf + hb).astype(yr.dtype)
    g = jax.nn.sigmoid(gl.reshape(B, T, N_BRANCH, D_MODEL))
    merged = g[:, :, 0] * (o_a @ lp['w_oa']) + g[:, :, 1] * (o_b @ lp['w_ob']) + g[:, :, 2] * (o_c @ lp['w_oc'])
    out = merged @ lp['w_out']
    if cache is None:
        return out, (ka, va, ckv, krope, hf[:, -1].astype(h.dtype), hb[:, 0].astype(h.dtype))
    return out, None


def conv_ffn(h, lp):
    up = dwconv(h @ lp['w_up'], lp['conv_ffn_w'], lp['conv_ffn_b'], FFN_PAD)
    val, gat = jnp.split(up, 2, axis=-1)
    return (jax.nn.gelu(gat) * val) @ lp['w_down']


def trunk_layer(x, mod, lp, angs, cache):
    sh1, sc1, gt1, sh2, sc2, gt2 = jnp.split(mod.astype(x.dtype), 6, axis=-1)
    h = rmsnorm(x, lp['g_pre_mix']) * (1 + sc1) + sh1
    a, st = mixer(h, lp, angs, cache)
    x = x + gt1 * rmsnorm(a, lp['g_post_mix'])
    h = rmsnorm(x, lp['g_pre_ffn']) * (1 + sc2) + sh2
    x = x + gt2 * rmsnorm(conv_ffn(h, lp), lp['g_post_ffn'])
    return x, st


def setup_inputs(seed: int = 0) -> dict:
    key = jax.random.key(seed)
    ks = list(jax.random.split(key, 40))
    f32 = jnp.float32

    def nrm(shape, scale):
        return jax.random.normal(ks.pop(), shape, f32) * scale

    def gain(shape):
        return 1.0 + nrm(shape, 0.01)

    u = jax.random.uniform(ks.pop(), (DEPTH, 2, D_RNN), f32, minval=0.9, maxval=0.999)
    s = u ** (1.0 / RG_C)
    lam = jnp.log(s) - jnp.log1p(-s)
    return {
        'x_prompt': nrm((BATCH, SEQ, D_MODEL), 1.0),
        'x_sample': nrm((DEC_BATCH, DEC_SEQ, D_MODEL), 1.0),
        'c': nrm((DEC_BATCH, D_MODEL), 1.0),
        'cache_gqa_k': nrm((DEC_BATCH, DEPTH, PAST_LEN, KV_A, HD_A), 1.0),
        'cache_gqa_v': nrm((DEC_BATCH, DEPTH, PAST_LEN, KV_A, HD_A), 1.0),
        'cache_mla_ckv': nrm((DEC_BATCH, DEPTH, PAST_LEN, KV_RANK), 1.0),
        'cache_mla_krope': nrm((DEC_BATCH, DEPTH, PAST_LEN, ROPE_B), 1.0),
        'state_rglru_fwd': nrm((DEC_BATCH, DEPTH, D_RNN), 0.5),
        'state_rglru_bwd': nrm((DEC_BATCH, DEPTH, D_RNN), 0.5),
        'c_ctx': nrm((D_MODEL,), 1.0),
        'w_ada': nrm((DEPTH, D_MODEL, 6 * D_MODEL), 0.5 * D_MODEL ** -0.5),
        'b_ada': nrm((DEPTH, 6 * D_MODEL), 0.02),
        'g_pre_mix': gain((DEPTH, D_MODEL)),
        'g_post_mix': gain((DEPTH, D_MODEL)),
        'g_pre_ffn': gain((DEPTH, D_MODEL)),
        'g_post_ffn': gain((DEPTH, D_MODEL)),
        'w_in': nrm((DEPTH, D_MODEL, IN_WIDTH), D_MODEL ** -0.5),
        'g_qa': gain((DEPTH, HD_A)),
        'g_ka': gain((DEPTH, HD_A)),
        'g_ckv': gain((DEPTH, KV_RANK)),
        'w_uk': nrm((DEPTH, KV_RANK, H_B * NOPE_B), KV_RANK ** -0.5),
        'w_uv': nrm((DEPTH, KV_RANK, H_B * VD_B), KV_RANK ** -0.5),
        'conv_rnn_w': nrm((DEPTH, RNN_CONV, D_RNN), RNN_CONV ** -0.5),
        'conv_rnn_b': nrm((DEPTH, D_RNN), 0.01),
        'w_rg': nrm((DEPTH, 2, RNN_BLOCKS, RNN_BS, RNN_BS), RNN_BS ** -0.5),
        'b_rg': nrm((DEPTH, 2, D_RNN), 0.01),
        'w_ig': nrm((DEPTH, 2, RNN_BLOCKS, RNN_BS, RNN_BS), RNN_BS ** -0.5),
        'b_ig': nrm((DEPTH, 2, D_RNN), 0.01),
        'lam': lam,
        'w_oa': nrm((DEPTH, QA_W, D_MODEL), QA_W ** -0.5),
        'w_ob': nrm((DEPTH, H_B * VD_B, D_MODEL), (H_B * VD_B) ** -0.5),
        'w_oc': nrm((DEPTH, D_RNN, D_MODEL), D_RNN ** -0.5),
        'w_out': nrm((DEPTH, D_MODEL, D_MODEL), D_MODEL ** -0.5),
        'w_up': nrm((DEPTH, D_MODEL, 2 * D_FF), D_MODEL ** -0.5),
        'conv_ffn_w': nrm((DEPTH, FFN_CONV, 2 * D_FF), FFN_CONV ** -0.5),
        'conv_ffn_b': nrm((DEPTH, 2 * D_FF), 0.01),
        'w_down': nrm((DEPTH, D_FF, D_MODEL), D_FF ** -0.5),
    }


def reference(x_prompt, x_sample, c, cache_gqa_k, cache_gqa_v, cache_mla_ckv, cache_mla_krope,
              state_rglru_fwd, state_rglru_bwd, c_ctx, w_ada, b_ada, g_pre_mix, g_post_mix,
              g_pre_ffn, g_post_ffn, w_in, g_qa, g_ka, g_ckv, w_uk, w_uv, conv_rnn_w, conv_rnn_b,
              w_rg, b_rg, w_ig, b_ig, lam, w_oa, w_ob, w_oc, w_out, w_up, conv_ffn_w, conv_ffn_b, w_down):
    def params(l):
        return {'g_pre_mix': g_pre_mix[l], 'g_post_mix': g_post_mix[l], 'g_pre_ffn': g_pre_ffn[l],
                'g_post_ffn': g_post_ffn[l], 'w_in': w_in[l], 'g_qa': g_qa[l], 'g_ka': g_ka[l],
                'g_ckv': g_ckv[l], 'w_uk': w_uk[l], 'w_uv': w_uv[l], 'conv_rnn_w': conv_rnn_w[l],
                'conv_rnn_b': conv_rnn_b[l], 'w_rg': w_rg[l], 'b_rg': b_rg[l], 'w_ig': w_ig[l],
                'b_ig': b_ig[l], 'lam': lam[l], 'w_oa': w_oa[l], 'w_ob': w_ob[l], 'w_oc': w_oc[l],
                'w_out': w_out[l], 'w_up': w_up[l], 'conv_ffn_w': conv_ffn_w[l],
                'conv_ffn_b': conv_ffn_b[l], 'w_down': w_down[l]}

    y = x_prompt
    states = []
    for l in range(DEPTH):
        mod = (jax.nn.silu(c_ctx) @ w_ada[l] + b_ada[l])[None, None, :]
        y, st = trunk_layer(y, mod, params(l), None, None)
        states.append(st)
    new_gqa_k = jnp.stack([s[0] for s in states], axis=1)
    new_gqa_v = jnp.stack([s[1] for s in states], axis=1)
    new_mla_ckv = jnp.stack([s[2] for s in states], axis=1)
    new_mla_krope = jnp.stack([s[3] for s in states], axis=1)
    new_rnn_fwd = jnp.stack([s[4] for s in states], axis=1)
    new_rnn_bwd = jnp.stack([s[5] for s in states], axis=1)

    rows = x_sample.shape[1] // GRID_W
    angs = (axial_angles(rows, HD_A), axial_angles(rows, ROPE_B))
    z = x_sample
    for l in range(DEPTH):
        mod = (jax.nn.silu(c) @ w_ada[l] + b_ada[l])[:, None, :]
        cache = (cache_gqa_k[:, l], cache_gqa_v[:, l], cache_mla_ckv[:, l], cache_mla_krope[:, l],
                 state_rglru_fwd[:, l], state_rglru_bwd[:, l])
        z, _ = trunk_layer(z, mod, params(l), angs, cache)

    return (y, z, new_gqa_k, new_gqa_v, new_mla_ckv, new_mla_krope, new_rnn_fwd, new_rnn_bwd)
```

```python
import functools

import jax
import jax.numpy as jnp
from jax import lax
from jax.experimental import pallas as pl
from jax.experimental.pallas import tpu as pltpu

F32 = jnp.float32
BF16 = jnp.bfloat16

D_MODEL = 1024
BATCH = 16
SEQ = 256
DEPTH = 2
DEC_BATCH = 4
DEC_SEQ = 2048
PAST_LEN = 512
GRID_W = 64
H_A = 8
KV_A = 2
HD_A = 64
H_B = 8
NOPE_B = 64
ROPE_B = 32
VD_B = 64
KV_RANK = 256
D_RNN = 1024
RNN_BLOCKS = 8
RNN_BS = D_RNN // RNN_BLOCKS
RG_C = 8.0
D_FF = 2816
ROPE_THETA = 10000.0
EPS = 1e-6
QA_W = H_A * HD_A
KA_W = KV_A * HD_A
QB_W = H_B * (NOPE_B + ROPE_B)

LANES = 128
N_CTX = BATCH * SEQ
N_LAT = DEC_BATCH * DEC_SEQ
N_TOK = N_CTX + N_LAT
S_LAT = PAST_LEN + DEC_SEQ
N_MOD_ROWS = 8
QB_PAD = H_B * LANES
KR_LANE = NOPE_B
OFF_QA, OFF_KA, OFF_VA, OFF_QB, OFF_CKV, OFF_KR = 0, 512, 640, 768, 1792, 2048
ATT_W = OFF_KR + LANES
CKR_W = KV_RANK + ROPE_B
TAB_W = 6 * LANES

TM_IN = 256
TM_KV = 512
TQ = 256
RNN_CB = 256
TM_MERGE = 512
TM_FFN = 512
TF_FFN = D_FF // 2
HALO = 16
VMEM_LIMIT = 56 * 1024 * 1024


def _cparams(n_axes):
    return pltpu.CompilerParams(dimension_semantics=("arbitrary",) * n_axes, vmem_limit_bytes=VMEM_LIMIT)


def _dot(a, b):
    return jnp.dot(a, b, preferred_element_type=F32)


def _rms(x, g):
    return x * lax.rsqrt(jnp.mean(x * x, axis=-1, keepdims=True) + EPS) * g


def _mod_row(i, tm):
    n_ctx_tiles = N_CTX // tm
    return jnp.where(i < n_ctx_tiles, 0, 1 + (i - n_ctx_tiles) // (DEC_SEQ // tm))


def _mod_kernel(c_ref, w_ref, b_ref, o_ref):
    c = c_ref[...]
    s = (c * jax.nn.sigmoid(c)).astype(BF16)
    o_ref[...] = _dot(s, w_ref[...].astype(BF16)) + b_ref[...]


def _modulation(cvec, w_ada, b_ada):
    tn = 1536
    return pl.pallas_call(
        _mod_kernel,
        grid=(DEPTH, 6 * D_MODEL // tn),
        in_specs=[
            pl.BlockSpec((N_MOD_ROWS, D_MODEL), lambda l, n: (0, 0)),
            pl.BlockSpec((None, D_MODEL, tn), lambda l, n: (l, 0, n)),
            pl.BlockSpec((None, 1, tn), lambda l, n: (l, 0, n)),
        ],
        out_specs=pl.BlockSpec((None, N_MOD_ROWS, tn), lambda l, n: (l, 0, n)),
        out_shape=jax.ShapeDtypeStruct((DEPTH, N_MOD_ROWS, 6 * D_MODEL), F32),
        compiler_params=_cparams(2),
        name="modulation",
    )(cvec, w_ada, b_ada.reshape(DEPTH, 1, 6 * D_MODEL))


def _seg_mean(x2, bd):
    hi = x2.astype(BF16)
    lo = (x2 - hi.astype(F32)).astype(BF16)
    return _dot(hi, bd) + _dot(lo, bd)


def _rope(x, cos, sin_up, sin_dn, shift):
    w = x.shape[-1]
    return x * cos + pltpu.roll(x, w - shift, 1) * sin_up + pltpu.roll(x, shift, 1) * sin_dn


def _inproj_kernel(x_ref, mod_ref, gpre_ref, tab_ref, w_ref, bd_ref, gqa_ref, gka_ref, gckv_ref,
                   qa_ref, qb_ref, ka_ref, va_ref, ckv_ref, kr_ref, h_ref):
    x = x_ref[...]
    sh1 = mod_ref[:, 0:D_MODEL]
    sc1 = mod_ref[:, D_MODEL:2 * D_MODEL]
    h = (_rms(x, gpre_ref[...]) * (1.0 + sc1) + sh1).astype(BF16)
    h_ref[...] = h
    y = _dot(h, w_ref[...])

    cos_a, sa_up, sa_dn = tab_ref[:, 0:128], tab_ref[:, 128:256], tab_ref[:, 256:384]
    cos_b, sb_up, sb_dn = tab_ref[:, 384:512], tab_ref[:, 512:640], tab_ref[:, 640:768]

    q = y[:, OFF_QA:OFF_QA + QA_W]
    q = q * lax.rsqrt(_seg_mean(q * q, bd_ref[...]) + EPS) * gqa_ref[...]
    rep = QA_W // LANES
    q = _rope(q, jnp.tile(cos_a, (1, rep)), jnp.tile(sa_up, (1, rep)), jnp.tile(sa_dn, (1, rep)), HD_A // 4)
    qa_ref[...] = (q * (HD_A ** -0.5)).astype(BF16)

    k = y[:, OFF_KA:OFF_KA + KA_W]
    k = k * lax.rsqrt(_seg_mean(k * k, bd_ref[0:KA_W, 0:KA_W]) + EPS) * gka_ref[...]
    ka_ref[...] = _rope(k, cos_a, sa_up, sa_dn, HD_A // 4)

    va_ref[...] = y[:, OFF_VA:OFF_VA + KA_W]

    qb = y[:, OFF_QB:OFF_QB + QB_PAD]
    qb = _rope(qb, jnp.tile(cos_b, (1, H_B)), jnp.tile(sb_up, (1, H_B)), jnp.tile(sb_dn, (1, H_B)), ROPE_B // 4)
    qb_ref[...] = qb.astype(BF16)

    ckv_ref[...] = _rms(y[:, OFF_CKV:OFF_CKV + KV_RANK], gckv_ref[...])

    kr = _rope(y[:, OFF_KR:OFF_KR + LANES], cos_b, sb_up, sb_dn, ROPE_B // 4)
    kr_ref[...] = kr[:, KR_LANE:KR_LANE + ROPE_B]


def _inproj(x, mods_l, gpre, tab, w_att, bd, gqa_t, gka_t, gckv):
    tm = TM_IN
    n_ctx_tiles = N_CTX // tm
    lat_tiles = DEC_SEQ // tm

    def tab_idx(i):
        return (jnp.where(i < n_ctx_tiles, 0, 1 + (i - n_ctx_tiles) % lat_tiles), 0)

    row = lambda i: (i, 0)
    const = lambda i: (0, 0)
    return pl.pallas_call(
        _inproj_kernel,
        grid=(N_TOK // tm,),
        in_specs=[
            pl.BlockSpec((tm, D_MODEL), row),
            pl.BlockSpec((None, 1, 6 * D_MODEL), lambda i: (_mod_row(i, tm), 0, 0)),
            pl.BlockSpec((1, D_MODEL), const),
            pl.BlockSpec((tm, TAB_W), tab_idx),
            pl.BlockSpec((D_MODEL, ATT_W), const),
            pl.BlockSpec((QA_W, QA_W), const),
            pl.BlockSpec((1, QA_W), const),
            pl.BlockSpec((1, KA_W), const),
            pl.BlockSpec((1, KV_RANK), const),
        ],
        out_specs=[
            pl.BlockSpec((tm, QA_W), row),
            pl.BlockSpec((tm, QB_PAD), row),
            pl.BlockSpec((tm, KA_W), row),
            pl.BlockSpec((tm, KA_W), row),
            pl.BlockSpec((tm, KV_RANK), row),
            pl.BlockSpec((tm, ROPE_B), row),
            pl.BlockSpec((tm, D_MODEL), row),
        ],
        out_shape=[
            jax.ShapeDtypeStruct((N_TOK, QA_W), BF16),
            jax.ShapeDtypeStruct((N_TOK, QB_PAD), BF16),
            jax.ShapeDtypeStruct((N_TOK, KA_W), F32),
            jax.ShapeDtypeStruct((N_TOK, KA_W), F32),
            jax.ShapeDtypeStruct((N_TOK, KV_RANK), F32),
            jax.ShapeDtypeStruct((N_TOK, ROPE_B), F32),
            jax.ShapeDtypeStruct((N_TOK, D_MODEL), BF16),
        ],
        compiler_params=_cparams(1),
        name="inproj",
    )(x, mods_l, gpre, tab, w_att, bd, gqa_t, gka_t, gckv)


def _kvup_kernel(c_ref, w_ref, k_ref, v_ref):
    y = _dot(c_ref[...], w_ref[...])
    for hd in range(H_B):
        k_ref[hd] = y[:, hd * LANES:(hd + 1) * LANES].astype(BF16)
        v_ref[hd] = y[:, QB_PAD + hd * LANES:QB_PAD + (hd + 1) * LANES].astype(BF16)


def _kvup(ckr_all, w_kv):
    rows = ckr_all.shape[0]
    tm = TM_KV
    return pl.pallas_call(
        _kvup_kernel,
        grid=(rows // tm,),
        in_specs=[
            pl.BlockSpec((tm, CKR_W), lambda i: (i, 0)),
            pl.BlockSpec((CKR_W, 2 * QB_PAD), lambda i: (0, 0)),
        ],
        out_specs=[
            pl.BlockSpec((H_B, tm, LANES), lambda i: (0, i, 0)),
            pl.BlockSpec((H_B, tm, LANES), lambda i: (0, i, 0)),
        ],
        out_shape=[
            jax.ShapeDtypeStruct((H_B, rows, LANES), BF16),
            jax.ShapeDtypeStruct((H_B, rows, LANES), BF16),
        ],
        compiler_params=_cparams(1),
        name="kvup",
    )(ckr_all, w_kv)


def _attn_kernel(qe_ref, qo_ref, ke_ref, ko_ref, ve_ref, vo_ref, o_ref, *, scale):
    def one(q_ref, k_ref, v_ref):
        s = lax.dot_general(q_ref[...], k_ref[...], (((1,), (1,)), ((), ())), preferred_element_type=F32)
        if scale is not None:
            s = s * scale
        m = jnp.max(s, axis=-1, keepdims=True)
        p = jnp.exp(s - m)
        l = jnp.sum(p, axis=-1, keepdims=True)
        return _dot(p.astype(BF16), v_ref[...]) / l

    o_ref[...] = (one(qe_ref, ke_ref, ve_ref) + one(qo_ref, ko_ref, vo_ref)).astype(o_ref.dtype)


def _attention(q, k, v, *, n_batch, t_len, s_len, tok0, key_blk0, qe_lane, qo_lane, ke_idx, ko_idx, scale, name):
    tq = TQ
    nq = t_len // tq
    q_blk0 = tok0 // tq

    def q_map(lane_fn):
        return lambda b, j, i: (q_blk0 + b * nq + i, lane_fn(j))

    def kv_map(idx_fn):
        return lambda b, j, i: (idx_fn(j), key_blk0 + b, 0)

    return pl.pallas_call(
        functools.partial(_attn_kernel, scale=scale),
        grid=(n_batch, 4, nq),
        in_specs=[
            pl.BlockSpec((tq, LANES), q_map(qe_lane)),
            pl.BlockSpec((tq, LANES), q_map(qo_lane)),
            pl.BlockSpec((None, s_len, LANES), kv_map(ke_idx)),
            pl.BlockSpec((None, s_len, LANES), kv_map(ko_idx)),
            pl.BlockSpec((None, s_len, LANES), kv_map(ke_idx)),
            pl.BlockSpec((None, s_len, LANES), kv_map(ko_idx)),
        ],
        out_specs=pl.BlockSpec((tq, LANES), lambda b, j, i: (b * nq + i, j)),
        out_shape=jax.ShapeDtypeStruct((n_batch * t_len, 4 * LANES), BF16),
        compiler_params=_cparams(3),
        name=name,
    )(q, q, k, k, v, v)


def _pair_variants(x):
    z = jnp.zeros_like(x[:, :HD_A])
    h0, h1 = x[:, :HD_A], x[:, HD_A:]
    cat = lambda a, b: jnp.concatenate([a, b], axis=-1)
    return jnp.stack([cat(h0, z), cat(z, h0), cat(h1, z), cat(z, h1)], axis=0)


def _one_minus_exp(x):
    e = jnp.exp(x)
    d = jnp.log(e)
    near = (1.0 - e) * x / jnp.where(d == 0.0, 1.0, d)
    near = jnp.where(d == 0.0, -x, near)
    return jnp.where(x > -0.5, near, 1.0 - e)


def _rglru_kernel(h_ref, wx_ref, wy_ref, cw_ref, cb_ref, wg_ref, bg_ref, lam_ref, h0_ref,
                  oc_ref, st_ref, af_ref, bf_ref, ab_ref, bb_ref, *, t_len):
    h = h_ref[...]
    xr = _dot(h, wx_ref[...])
    rows = lax.broadcasted_iota(jnp.int32, (t_len, 1), 0)
    x_m2 = jnp.where(rows >= 2, pltpu.roll(xr, 2, 0), 0.0)
    x_m1 = jnp.where(rows >= 1, pltpu.roll(xr, 1, 0), 0.0)
    x_p1 = jnp.where(rows < t_len - 1, pltpu.roll(xr, t_len - 1, 0), 0.0)
    u = cw_ref[0:1, :] * x_m2 + cw_ref[1:2, :] * x_m1 + cw_ref[2:3, :] * xr + cw_ref[3:4, :] * x_p1 + cb_ref[...]

    lam = lam_ref[...]
    log_sig = jnp.minimum(lam, 0.0) - jnp.log1p(jnp.exp(-jnp.abs(lam)))
    for j in range(RNN_CB // RNN_BS):
        sl = slice(j * RNN_BS, (j + 1) * RNN_BS)
        uj = u[:, sl]
        g = _dot(uj.astype(BF16), wg_ref[j])
        for d, (a_ref, b_ref) in enumerate(((af_ref, bf_ref), (ab_ref, bb_ref))):
            r = jax.nn.sigmoid(g[:, (2 * d) * RNN_BS:(2 * d + 1) * RNN_BS] + bg_ref[2 * d:2 * d + 1, sl])
            i = jax.nn.sigmoid(g[:, (2 * d + 1) * RNN_BS:(2 * d + 2) * RNN_BS] + bg_ref[2 * d + 1:2 * d + 2, sl])
            log_a = RG_C * r * log_sig[d:d + 1, sl]
            a_ref[:, sl] = jnp.exp(log_a)
            b_ref[:, sl] = jnp.sqrt(_one_minus_exp(2.0 * log_a)) * (i * uj)

    def step(t, carry):
        hf, hb = carry
        tb = t_len - 1 - t
        hf = af_ref[pl.ds(t, 1), :] * hf + bf_ref[pl.ds(t, 1), :]
        bf_ref[pl.ds(t, 1), :] = hf
        hb = ab_ref[pl.ds(tb, 1), :] * hb + bb_ref[pl.ds(tb, 1), :]
        bb_ref[pl.ds(tb, 1), :] = hb
        return hf, hb

    hf, hb = lax.fori_loop(0, t_len, step, (h0_ref[0:1, :], h0_ref[1:2, :]), unroll=8)
    st_ref[0:1, :] = hf
    st_ref[1:2, :] = hb
    yr = _dot(h, wy_ref[...])
    oc_ref[...] = (jax.nn.gelu(yr) * (bf_ref[...] + bb_ref[...])).astype(BF16)


def _rglru(h_all, w_xr, w_yr, conv_w, conv_b, wg, bg, lam, h0, *, n_seq, t_len, tok0, name):
    cb = RNN_CB
    seq_blk0 = tok0 // t_len
    chan = lambda b, n: (0, n)
    return pl.pallas_call(
        functools.partial(_rglru_kernel, t_len=t_len),
        grid=(n_seq, D_RNN // cb),
        in_specs=[
            pl.BlockSpec((t_len, D_MODEL), lambda b, n: (seq_blk0 + b, 0)),
            pl.BlockSpec((D_MODEL, cb), chan),
            pl.BlockSpec((D_MODEL, cb), chan),
            pl.BlockSpec((4, cb), chan),
            pl.BlockSpec((1, cb), chan),
            pl.BlockSpec((cb // RNN_BS, RNN_BS, 4 * RNN_BS), lambda b, n: (n, 0, 0)),
            pl.BlockSpec((4, cb), chan),
            pl.BlockSpec((2, cb), chan),
            pl.BlockSpec((None, 2, cb), lambda b, n: (b, 0, n)),
        ],
        out_specs=[
            pl.BlockSpec((t_len, cb), lambda b, n: (b, n)),
            pl.BlockSpec((None, 2, cb), lambda b, n: (b, 0, n)),
        ],
        out_shape=[
            jax.ShapeDtypeStruct((n_seq * t_len, D_RNN), BF16),
            jax.ShapeDtypeStruct((n_seq, 2, D_RNN), F32),
        ],
        scratch_shapes=[pltpu.VMEM((t_len, cb), F32)] * 4,
        compiler_params=_cparams(2),
        name=name,
    )(h_all, w_xr, w_yr, conv_w, conv_b, wg, bg, lam, h0)


def _merge_kernel(oa_ref, ob_ref, oc_ref, h_ref, x_ref, mod_ref, gpost_ref, gpre2_ref,
                  woa_ref, wob_ref, woc_ref, wgl_ref, wout_ref, x1_ref, h2_ref):
    h = h_ref[...]

    def gate(k):
        return jax.nn.sigmoid(_dot(h, wgl_ref[:, k * D_MODEL:(k + 1) * D_MODEL]))

    merged = gate(0) * _dot(oa_ref[...], woa_ref[...])
    merged = merged + gate(1) * _dot(ob_ref[...], wob_ref[...])
    merged = merged + gate(2) * _dot(oc_ref[...], woc_ref[...])
    out = _dot(merged.astype(BF16), wout_ref[...])
    gt1 = mod_ref[:, 2 * D_MODEL:3 * D_MODEL]
    sh2 = mod_ref[:, 3 * D_MODEL:4 * D_MODEL]
    sc2 = mod_ref[:, 4 * D_MODEL:5 * D_MODEL]
    x1 = x_ref[...] + gt1 * _rms(out, gpost_ref[...])
    x1_ref[...] = x1
    h2_ref[...] = (_rms(x1, gpre2_ref[...]) * (1.0 + sc2) + sh2).astype(BF16)


def _merge(o_a, o_b, o_c, h, x, mods_l, gpost, gpre2, w_oa, w_ob, w_oc, w_gl, w_out):
    tm = TM_MERGE
    row = lambda i: (i, 0)
    const = lambda i: (0, 0)
    return pl.pallas_call(
        _merge_kernel,
        grid=(N_TOK // tm,),
        in_specs=[
            pl.BlockSpec((tm, QA_W), row),
            pl.BlockSpec((tm, H_B * VD_B), row),
            pl.BlockSpec((tm, D_RNN), row),
            pl.BlockSpec((tm, D_MODEL), row),
            pl.BlockSpec((tm, D_MODEL), row),
            pl.BlockSpec((None, 1, 6 * D_MODEL), lambda i: (_mod_row(i, tm), 0, 0)),
            pl.BlockSpec((1, D_MODEL), const),
            pl.BlockSpec((1, D_MODEL), const),
            pl.BlockSpec((QA_W, D_MODEL), const),
            pl.BlockSpec((H_B * VD_B, D_MODEL), const),
            pl.BlockSpec((D_RNN, D_MODEL), const),
            pl.BlockSpec((D_MODEL, 3 * D_MODEL), const),
            pl.BlockSpec((D_MODEL, D_MODEL), const),
        ],
        out_specs=[pl.BlockSpec((tm, D_MODEL), row), pl.BlockSpec((tm, D_MODEL), row)],
        out_shape=[
            jax.ShapeDtypeStruct((N_TOK, D_MODEL), F32),
            jax.ShapeDtypeStruct((N_TOK, D_MODEL), BF16),
        ],
        compiler_params=_cparams(1),
        name="merge",
    )(o_a, o_b, o_c, h, x, mods_l, gpost, gpre2, w_oa, w_ob, w_oc, w_gl, w_out)


def _ffn_kernel(hp_ref, hm_ref, hn_ref, x1_ref, mod_ref, gpost_ref, wv_ref, wg_ref, cwv_ref, cwg_ref,
                cbv_ref, cbg_ref, wd_ref, o_ref, hext_ref, acc_ref):
    i = pl.program_id(0)
    c = pl.program_id(1)
    tm = TM_FFN

    @pl.when(c == 0)
    def _():
        hext_ref[0:HALO, :] = hp_ref[...]
        hext_ref[HALO:HALO + tm, :] = hm_ref[...]
        hext_ref[HALO + tm:, :] = hn_ref[...]
        acc_ref[...] = jnp.zeros_like(acc_ref)

    r = i * tm + lax.broadcasted_iota(jnp.int32, (tm, 1), 0)
    t = jnp.where(r < N_CTX, r & (SEQ - 1), r & (DEC_SEQ - 1))
    t_last = jnp.where(r < N_CTX, SEQ - 1, DEC_SEQ - 1)
    first = t == 0
    last = t == t_last
    hext = hext_ref[...]
    n_ext = tm + 2 * HALO

    def conv(w_ref, cw_ref, cb_ref):
        up = _dot(hext, w_ref[...])
        prev = jnp.where(first, 0.0, pltpu.roll(up, 1, 0)[HALO:HALO + tm])
        nxt = jnp.where(last, 0.0, pltpu.roll(up, n_ext - 1, 0)[HALO:HALO + tm])
        return cw_ref[0:1, :] * prev + cw_ref[1:2, :] * up[HALO:HALO + tm] + cw_ref[2:3, :] * nxt + cb_ref[...]

    val = conv(wv_ref, cwv_ref, cbv_ref)
    gat = conv(wg_ref, cwg_ref, cbg_ref)
    acc_ref[...] += _dot((jax.nn.gelu(gat) * val).astype(BF16), wd_ref[...])

    @pl.when(c == pl.num_programs(1) - 1)
    def _():
        gt2 = mod_ref[:, 5 * D_MODEL:6 * D_MODEL]
        o_ref[...] = x1_ref[...] + gt2 * _rms(acc_ref[...], gpost_ref[...])


def _ffn(h2, x1, mods_l, gpost, w_up, conv_w, conv_b, w_down):
    tm, tf = TM_FFN, TF_FFN
    nc = D_FF // tf
    per = tm // HALO
    n_halo_blocks = N_TOK // HALO
    row = lambda i, c: (i, 0)
    return pl.pallas_call(
        _ffn_kernel,
        grid=(N_TOK // tm, nc),
        in_specs=[
            pl.BlockSpec((HALO, D_MODEL), lambda i, c: (jnp.maximum(i * per - 1, 0), 0)),
            pl.BlockSpec((tm, D_MODEL), row),
            pl.BlockSpec((HALO, D_MODEL), lambda i, c: (jnp.minimum((i + 1) * per, n_halo_blocks - 1), 0)),
            pl.BlockSpec((tm, D_MODEL), row),
            pl.BlockSpec((None, 1, 6 * D_MODEL), lambda i, c: (_mod_row(i, tm), 0, 0)),
            pl.BlockSpec((1, D_MODEL), lambda i, c: (0, 0)),
            pl.BlockSpec((D_MODEL, tf), lambda i, c: (0, c)),
            pl.BlockSpec((D_MODEL, tf), lambda i, c: (0, nc + c)),
            pl.BlockSpec((3, tf), lambda i, c: (0, c)),
            pl.BlockSpec((3, tf), lambda i, c: (0, nc + c)),
            pl.BlockSpec((1, tf), lambda i, c: (0, c)),
            pl.BlockSpec((1, tf), lambda i, c: (0, nc + c)),
            pl.BlockSpec((tf, D_MODEL), lambda i, c: (c, 0)),
        ],
        out_specs=pl.BlockSpec((tm, D_MODEL), row),
        out_shape=jax.ShapeDtypeStruct((N_TOK, D_MODEL), F32),
        scratch_shapes=[pltpu.VMEM((tm + 2 * HALO, D_MODEL), BF16), pltpu.VMEM((tm, D_MODEL), F32)],
        compiler_params=_cparams(2),
        name="ffn",
    )(h2, h2, h2, x1, mods_l, gpost, w_up, w_up, conv_w, conv_w, conv_b, conv_b, w_down)


def _rope_tables():
    t = jnp.arange(DEC_SEQ)
    row = (t // GRID_W).astype(F32)[:, None]
    col = (t % GRID_W).astype(F32)[:, None]

    def parts(dim):
        n = dim // 4
        inv = ROPE_THETA ** (-jnp.arange(n, dtype=F32) / n)
        ar, ac = row * inv, col * inv
        z = jnp.zeros_like(ar)
        cos = jnp.concatenate([jnp.cos(ar), jnp.cos(ar), jnp.cos(ac), jnp.cos(ac)], axis=-1)
        s_up = jnp.concatenate([-jnp.sin(ar), z, -jnp.sin(ac), z], axis=-1)
        s_dn = jnp.concatenate([z, jnp.sin(ar), z, jnp.sin(ac)], axis=-1)
        return cos, s_up, s_dn

    a = [jnp.tile(p, (1, LANES // HD_A)) for p in parts(HD_A)]
    one = jnp.ones((DEC_SEQ, 1), F32)
    pad = lambda p, fill: jnp.concatenate(
        [fill * jnp.ones((DEC_SEQ, KR_LANE), F32), p, fill * jnp.ones((DEC_SEQ, LANES - KR_LANE - ROPE_B), F32)], axis=-1)
    cb, sbu, sbd = parts(ROPE_B)
    b = [pad(cb, 1.0), pad(sbu, 0.0), pad(sbd, 0.0)]
    del one
    lat = jnp.concatenate(a + b, axis=-1)
    ident_blk = jnp.concatenate([jnp.ones((TM_IN, LANES), F32), jnp.zeros((TM_IN, 2 * LANES), F32)], axis=-1)
    ident = jnp.concatenate([ident_blk, ident_blk], axis=-1)
    return jnp.concatenate([ident, lat], axis=0)


def _pack_w_att(w_in):
    qa = w_in[:, 0:512]
    ka = w_in[:, 512:640]
    va = w_in[:, 640:768]
    qb = w_in[:, 768:1536].reshape(D_MODEL, H_B, NOPE_B + ROPE_B)
    qb = jnp.pad(qb, ((0, 0), (0, 0), (0, LANES - NOPE_B - ROPE_B))).reshape(D_MODEL, QB_PAD)
    ckv = w_in[:, 1536:1792]
    kr = jnp.pad(w_in[:, 1792:1824], ((0, 0), (KR_LANE, LANES - KR_LANE - ROPE_B)))
    return jnp.concatenate([qa, ka, va, qb, ckv, kr], axis=-1).astype(BF16)


def _pack_w_kv(w_uk, w_uv):
    uk = w_uk.reshape(KV_RANK, H_B, NOPE_B)
    k_top = jnp.pad(uk, ((0, 0), (0, 0), (0, LANES - NOPE_B))).reshape(KV_RANK, QB_PAD)
    place = jnp.pad(jnp.eye(ROPE_B, dtype=F32), ((0, 0), (KR_LANE, LANES - KR_LANE - ROPE_B)))
    k_bot = jnp.tile(place, (1, H_B))
    uv = w_uv.reshape(KV_RANK, H_B // 2, 2, VD_B)
    z = jnp.zeros_like(uv[:, :, 0])
    v_even = jnp.concatenate([uv[:, :, 0], z], axis=-1)
    v_odd = jnp.concatenate([z, uv[:, :, 1]], axis=-1)
    v_top = jnp.stack([v_even, v_odd], axis=2).reshape(KV_RANK, QB_PAD)
    top = jnp.concatenate([k_top, v_top], axis=-1)
    bot = jnp.concatenate([k_bot, jnp.zeros((ROPE_B, QB_PAD), F32)], axis=-1)
    return jnp.concatenate([top, bot], axis=0).astype(BF16)


def _pack_gate_w(w_rg, w_ig):
    return jnp.concatenate([w_rg[0], w_ig[0], w_rg[1], w_ig[1]], axis=-1).astype(BF16)


def kernel(x_prompt, x_sample, c, cache_gqa_k, cache_gqa_v, cache_mla_ckv, cache_mla_krope, state_rglru_fwd, state_rglru_bwd, c_ctx, w_ada, b_ada, g_pre_mix, g_post_mix, g_pre_ffn, g_post_ffn, w_in, g_qa, g_ka, g_ckv, w_uk, w_uv, conv_rnn_w, conv_rnn_b, w_rg, b_rg, w_ig, b_ig, lam, w_oa, w_ob, w_oc, w_out, w_up, conv_ffn_w, conv_ffn_b, w_down):
    x = jnp.concatenate([x_prompt.reshape(N_CTX, D_MODEL), x_sample.reshape(N_LAT, D_MODEL)], axis=0)
    cvec = jnp.concatenate([c_ctx[None, :], c, jnp.zeros((N_MOD_ROWS - 1 - DEC_BATCH, D_MODEL), F32)], axis=0)
    mods = _modulation(cvec, w_ada, b_ada).reshape(DEPTH * N_MOD_ROWS, 1, 6 * D_MODEL)

    tab = _rope_tables()
    seg = jnp.arange(QA_W) // HD_A
    bd = jnp.where(seg[:, None] == seg[None, :], 1.0 / HD_A, 0.0).astype(BF16)
    zero_state = jnp.zeros((BATCH, 2, D_RNN), F32)

    new_k, new_v, new_ckv, new_kr, new_fwd, new_bwd = [], [], [], [], [], []
    for l in range(DEPTH):
        mods_l = mods[l * N_MOD_ROWS:(l + 1) * N_MOD_ROWS]
        w_att = _pack_w_att(w_in[l])
        w_xr = w_in[l][:, 1824:2848].astype(BF16)
        w_yr = w_in[l][:, 2848:3872].astype(BF16)
        w_gl = w_in[l][:, 3872:6944].astype(BF16)

        qa, qb, ka, va, ckv, kr, h = _inproj(
            x, mods_l, g_pre_mix[l][None, :], tab, w_att, bd,
            jnp.tile(g_qa[l], H_A)[None, :], jnp.tile(g_ka[l], KV_A)[None, :], g_ckv[l][None, :])

        def lat_keys(cache, new):
            return jnp.concatenate([cache.reshape(DEC_BATCH, PAST_LEN, -1).astype(BF16),
                                    new[N_CTX:].astype(BF16).reshape(DEC_BATCH, DEC_SEQ, -1)], axis=1)

        ka_lat = lat_keys(cache_gqa_k[:, l], ka).reshape(DEC_BATCH * S_LAT, KA_W)
        va_lat = lat_keys(cache_gqa_v[:, l], va).reshape(DEC_BATCH * S_LAT, KA_W)
        gqa = dict(qe_lane=lambda j: j, qo_lane=lambda j: j, ke_idx=lambda j: 2 * (j // 2),
                   ko_idx=lambda j: 2 * (j // 2) + 1, scale=None)
        oa_ctx = _attention(qa, _pair_variants(ka[:N_CTX].astype(BF16)), _pair_variants(va[:N_CTX].astype(BF16)),
                            n_batch=BATCH, t_len=SEQ, s_len=SEQ, tok0=0, key_blk0=0, name="gqa_ctx", **gqa)
        oa_lat = _attention(qa, _pair_variants(ka_lat), _pair_variants(va_lat),
                            n_batch=DEC_BATCH, t_len=DEC_SEQ, s_len=S_LAT, tok0=N_CTX, key_blk0=0, name="gqa_lat", **gqa)
        o_a = jnp.concatenate([oa_ctx, oa_lat], axis=0)

        ckr = jnp.concatenate([ckv, kr], axis=-1)
        ckr_cache = jnp.concatenate([cache_mla_ckv[:, l], cache_mla_krope[:, l]], axis=-1)
        ckr_all = jnp.concatenate([lat_keys(ckr_cache, ckr).reshape(DEC_BATCH * S_LAT, CKR_W),
                                   ckr[:N_CTX].astype(BF16)], axis=0)
        kb, vb = _kvup(ckr_all, _pack_w_kv(w_uk[l], w_uv[l]))
        mla = dict(qe_lane=lambda j: 2 * j, qo_lane=lambda j: 2 * j + 1, ke_idx=lambda j: 2 * j,
                   ko_idx=lambda j: 2 * j + 1, scale=(NOPE_B + ROPE_B) ** -0.5)
        ob_ctx = _attention(qb, kb, vb, n_batch=BATCH, t_len=SEQ, s_len=SEQ, tok0=0,
                            key_blk0=DEC_BATCH * S_LAT // SEQ, name="mla_ctx", **mla)
        ob_lat = _attention(qb, kb, vb, n_batch=DEC_BATCH, t_len=DEC_SEQ, s_len=S_LAT, tok0=N_CTX,
                            key_blk0=0, name="mla_lat", **mla)
        o_b = jnp.concatenate([ob_ctx, ob_lat], axis=0)

        wg = _pack_gate_w(w_rg[l], w_ig[l])
        bg = jnp.stack([b_rg[l][0], b_ig[l][0], b_rg[l][1], b_ig[l][1]], axis=0)
        rnn_args = (w_xr, w_yr, conv_rnn_w[l], conv_rnn_b[l][None, :], wg, bg, lam[l])
        oc_ctx, st_ctx = _rglru(h, *rnn_args, zero_state, n_seq=BATCH, t_len=SEQ, tok0=0, name="rglru_ctx")
        h0_lat = jnp.stack([state_rglru_fwd[:, l], state_rglru_bwd[:, l]], axis=1)
        oc_lat, _ = _rglru(h, *rnn_args, h0_lat, n_seq=DEC_BATCH, t_len=DEC_SEQ, tok0=N_CTX, name="rglru_lat")
        o_c = jnp.concatenate([oc_ctx, oc_lat], axis=0)

        x1, h2 = _merge(o_a, o_b, o_c, h, x, mods_l, g_post_mix[l][None, :], g_pre_ffn[l][None, :],
                        w_oa[l].astype(BF16), w_ob[l].astype(BF16), w_oc[l].astype(BF16), w_gl, w_out[l].astype(BF16))
        x = _ffn(h2, x1, mods_l, g_post_ffn[l][None, :], w_up[l].astype(BF16), conv_ffn_w[l],
                 conv_ffn_b[l][None, :], w_down[l].astype(BF16))

        new_k.append(ka[:N_CTX].reshape(BATCH, SEQ, KV_A, HD_A))
        new_v.append(va[:N_CTX].reshape(BATCH, SEQ, KV_A, HD_A))
        new_ckv.append(ckv[:N_CTX].reshape(BATCH, SEQ, KV_RANK))
        new_kr.append(kr[:N_CTX].reshape(BATCH, SEQ, ROPE_B))
        new_fwd.append(st_ctx[:, 0])
        new_bwd.append(st_ctx[:, 1])

    y = x[:N_CTX].reshape(BATCH, SEQ, D_MODEL)
    z = x[N_CTX:].reshape(DEC_BATCH, DEC_SEQ, D_MODEL)
    stack = lambda xs: jnp.stack(xs, axis=1)
    return (y, z, stack(new_k), stack(new_v), stack(new_ckv), stack(new_kr), stack(new_fwd), stack(new_bwd))
```

```python
import functools
import math

import jax
import jax.numpy as jnp
from jax import lax
from jax.experimental import pallas as pl
from jax.experimental.pallas import tpu as pltpu

F32 = jnp.float32
BF16 = jnp.bfloat16

D_MODEL = 1024
BATCH = 16
SEQ = 256
DEPTH = 2
DEC_BATCH = 4
DEC_SEQ = 2048
PAST_LEN = 512
GRID_W = 64
H_A = 8
KV_A = 2
HD_A = 64
H_B = 8
NOPE_B = 64
ROPE_B = 32
VD_B = 64
KV_RANK = 256
D_RNN = 1024
RNN_BLOCKS = 8
RNN_BS = D_RNN // RNN_BLOCKS
RG_C = 8.0
D_FF = 2816
ROPE_THETA = 10000.0
EPS = 1e-6
QA_W = H_A * HD_A
KA_W = KV_A * HD_A
QB_W = H_B * (NOPE_B + ROPE_B)
OB_W = H_B * VD_B

LANES = 128
SUBLANES = 8
N_CTX = BATCH * SEQ
N_LAT = DEC_BATCH * DEC_SEQ
N_TOK = N_CTX + N_LAT
N_MOD_ROWS = 8
QB_PAD = H_B * LANES
KR_LANE = NOPE_B
N_PAIR = 4
OFF_QA, OFF_KA, OFF_VA, OFF_QB, OFF_CKV, OFF_KR = 0, 512, 640, 768, 1792, 2048
ATT_W = OFF_KR + LANES
KV_W = QB_PAD + OB_W
TAB_W = 6 * LANES
LOG2E = math.log2(math.e)

TM_IN = 256
TQ = 256
RNN_CB = 256
TM_MERGE = 512
TM_FFN = 512
TF_FFN = D_FF // 2
HALO = 16
VMEM_LIMIT = 56 * 1024 * 1024


def _cparams(n_axes):
    return pltpu.CompilerParams(dimension_semantics=("arbitrary",) * n_axes, vmem_limit_bytes=VMEM_LIMIT)


def _dot(a, b):
    return jnp.dot(a, b, preferred_element_type=F32)


def _rms(x, g):
    return x * lax.rsqrt(jnp.mean(x * x, axis=-1, keepdims=True) + EPS) * g


def _mod_row(l, i, tm):
    n_ctx_tiles = N_CTX // tm
    return l * N_MOD_ROWS + jnp.where(i < n_ctx_tiles, 0, 1 + (i - n_ctx_tiles) // (DEC_SEQ // tm))


def _mod_kernel(c_ref, w_ref, b_ref, o_ref):
    c = c_ref[...]
    s = (c * jax.nn.sigmoid(c)).astype(BF16)
    o_ref[...] = _dot(s, w_ref[...].astype(BF16)) + b_ref[...]


def _modulation(cvec, w_ada, b_ada):
    tn = 1536
    return pl.pallas_call(
        _mod_kernel,
        grid=(DEPTH, 6 * D_MODEL // tn),
        in_specs=[
            pl.BlockSpec((N_MOD_ROWS, D_MODEL), lambda l, n: (0, 0)),
            pl.BlockSpec((None, D_MODEL, tn), lambda l, n: (l, 0, n)),
            pl.BlockSpec((None, 1, tn), lambda l, n: (l, 0, n)),
        ],
        out_specs=pl.BlockSpec((None, N_MOD_ROWS, tn), lambda l, n: (l, 0, n)),
        out_shape=jax.ShapeDtypeStruct((DEPTH, N_MOD_ROWS, 6 * D_MODEL), F32),
        compiler_params=_cparams(2),
        name="modulation",
    )(cvec, w_ada, b_ada.reshape(DEPTH, 1, 6 * D_MODEL))


def _seg_mean(x2, bd):
    hi = x2.astype(BF16)
    lo = (x2 - hi.astype(F32)).astype(BF16)
    return _dot(hi, bd) + _dot(lo, bd)


def _rope(x, cos, sin_up, sin_dn, shift):
    w = x.shape[-1]
    return x * cos + pltpu.roll(x, w - shift, 1) * sin_up + pltpu.roll(x, shift, 1) * sin_dn


def _dup_heads(x):
    lo = lax.broadcasted_iota(jnp.int32, x.shape, 1) < HD_A
    sw = pltpu.roll(x, HD_A, 1)
    return jnp.where(lo, x, sw).astype(BF16), jnp.where(lo, sw, x).astype(BF16)


def _ones_column(rows):
    return jnp.where(lax.broadcasted_iota(jnp.int32, (rows, LANES), 1) == 0, 1.0, 0.0).astype(BF16)


def _store_mla_kv(y, kr_all, ones, k_ref, v_ref):
    for hd in range(H_B):
        sl = slice(hd * LANES, (hd + 1) * LANES)
        k_ref[hd] = (y[:, sl] + kr_all[:, sl]).astype(BF16)
    for j in range(N_PAIR):
        v_ref[j, :, 0:LANES] = y[:, QB_PAD + j * LANES:QB_PAD + (j + 1) * LANES].astype(BF16)
        v_ref[j, :, LANES:2 * LANES] = ones


def _inproj_kernel(x_ref, mod_ref, gpre_ref, tab_ref, w_ref, bd_ref, gqa_ref, gka_ref, gckv_ref, wkv_ref, place_ref,
                   qa_ref, qb_ref, ka_ref, va_ref, kdup_ref, vdup_ref, ckv_ref, kr_ref, kb_ref, vb_ref, h_ref):
    x = x_ref[...]
    sh1 = mod_ref[:, 0:D_MODEL]
    sc1 = mod_ref[:, D_MODEL:2 * D_MODEL]
    h = (_rms(x, gpre_ref[...]) * (1.0 + sc1) + sh1).astype(BF16)
    h_ref[...] = h
    y = _dot(h, w_ref[...])

    cos_a, sa_up, sa_dn = tab_ref[:, 0:128], tab_ref[:, 128:256], tab_ref[:, 256:384]
    cos_b, sb_up, sb_dn = tab_ref[:, 384:512], tab_ref[:, 512:640], tab_ref[:, 640:768]

    q = y[:, OFF_QA:OFF_QA + QA_W]
    q = q * lax.rsqrt(_seg_mean(q * q, bd_ref[...]) + EPS) * gqa_ref[...]
    rep = QA_W // LANES
    q = _rope(q, jnp.tile(cos_a, (1, rep)), jnp.tile(sa_up, (1, rep)), jnp.tile(sa_dn, (1, rep)), HD_A // 4)
    qa_ref[...] = (q * (HD_A ** -0.5)).astype(BF16)

    k = y[:, OFF_KA:OFF_KA + KA_W]
    k = k * lax.rsqrt(_seg_mean(k * k, bd_ref[0:KA_W, 0:KA_W]) + EPS) * gka_ref[...]
    k = _rope(k, cos_a, sa_up, sa_dn, HD_A // 4)
    ka_ref[...] = k
    v = y[:, OFF_VA:OFF_VA + KA_W]
    va_ref[...] = v
    ones = _ones_column(k.shape[0])
    for n, (kd, vd) in enumerate(zip(_dup_heads(k), _dup_heads(v))):
        kdup_ref[n] = kd
        vdup_ref[n, :, 0:LANES] = vd
        vdup_ref[n, :, LANES:2 * LANES] = ones

    qb = y[:, OFF_QB:OFF_QB + QB_PAD]
    qb = _rope(qb, jnp.tile(cos_b, (1, H_B)), jnp.tile(sb_up, (1, H_B)), jnp.tile(sb_dn, (1, H_B)), ROPE_B // 4)
    qb_ref[...] = qb.astype(BF16)

    ckv = _rms(y[:, OFF_CKV:OFF_CKV + KV_RANK], gckv_ref[...])
    ckv_ref[...] = ckv
    kr = _rope(y[:, OFF_KR:OFF_KR + LANES], cos_b, sb_up, sb_dn, ROPE_B // 4)
    kr_ref[...] = kr[:, KR_LANE:KR_LANE + ROPE_B]

    y2 = _dot(ckv.astype(BF16), wkv_ref[...])
    kr_all = _dot(kr.astype(BF16), place_ref[...])
    _store_mla_kv(y2, kr_all, ones, kb_ref, vb_ref)


def _inproj(l, x, mods, gpre, tab, w_att, bd, gqa_t, gka_t, gckv, w_kv, place):
    tm = TM_IN
    n_ctx_tiles = N_CTX // tm
    lat_tiles = DEC_SEQ // tm

    def tab_idx(i):
        return (jnp.where(i < n_ctx_tiles, 0, 1 + (i - n_ctx_tiles) % lat_tiles), 0)

    row = lambda i: (i, 0)
    row3 = lambda i: (0, i, 0)
    const = lambda i: (0, 0)
    layer = lambda i: (l, 0, 0)
    return pl.pallas_call(
        _inproj_kernel,
        grid=(N_TOK // tm,),
        in_specs=[
            pl.BlockSpec((tm, D_MODEL), row),
            pl.BlockSpec((None, 1, 6 * D_MODEL), lambda i: (_mod_row(l, i, tm), 0, 0)),
            pl.BlockSpec((None, 1, D_MODEL), layer),
            pl.BlockSpec((tm, TAB_W), tab_idx),
            pl.BlockSpec((None, D_MODEL, ATT_W), layer),
            pl.BlockSpec((QA_W, QA_W), const),
            pl.BlockSpec((None, 1, QA_W), layer),
            pl.BlockSpec((None, 1, KA_W), layer),
            pl.BlockSpec((None, 1, KV_RANK), layer),
            pl.BlockSpec((None, KV_RANK, KV_W), layer),
            pl.BlockSpec((LANES, QB_PAD), const),
        ],
        out_specs=[
            pl.BlockSpec((tm, QA_W), row),
            pl.BlockSpec((tm, QB_PAD), row),
            pl.BlockSpec((tm, KA_W), row),
            pl.BlockSpec((tm, KA_W), row),
            pl.BlockSpec((KV_A, tm, LANES), row3),
            pl.BlockSpec((KV_A, tm, 2 * LANES), row3),
            pl.BlockSpec((tm, KV_RANK), row),
            pl.BlockSpec((tm, ROPE_B), row),
            pl.BlockSpec((H_B, tm, LANES), row3),
            pl.BlockSpec((N_PAIR, tm, 2 * LANES), row3),
            pl.BlockSpec((tm, D_MODEL), row),
        ],
        out_shape=[
            jax.ShapeDtypeStruct((N_TOK, QA_W), BF16),
            jax.ShapeDtypeStruct((N_TOK, QB_PAD), BF16),
            jax.ShapeDtypeStruct((N_TOK, KA_W), F32),
            jax.ShapeDtypeStruct((N_TOK, KA_W), F32),
            jax.ShapeDtypeStruct((KV_A, N_TOK, LANES), BF16),
            jax.ShapeDtypeStruct((KV_A, N_TOK, 2 * LANES), BF16),
            jax.ShapeDtypeStruct((N_TOK, KV_RANK), F32),
            jax.ShapeDtypeStruct((N_TOK, ROPE_B), F32),
            jax.ShapeDtypeStruct((H_B, N_TOK, LANES), BF16),
            jax.ShapeDtypeStruct((N_PAIR, N_TOK, 2 * LANES), BF16),
            jax.ShapeDtypeStruct((N_TOK, D_MODEL), BF16),
        ],
        compiler_params=_cparams(1),
        name="inproj",
    )(x, mods, gpre, tab, w_att, bd, gqa_t, gka_t, gckv, w_kv, place)


def _kvup_cache_kernel(c_ref, r_ref, w_ref, place_ref, k_ref, v_ref):
    y = _dot(c_ref[...].astype(BF16), w_ref[...])
    kr_all = _dot(r_ref[...].astype(BF16), place_ref[...])
    _store_mla_kv(y, kr_all, _ones_column(y.shape[0]), k_ref, v_ref)


def _kvup_cache(cache_ckv, cache_kr, w_kv, place32):
    rows = DEC_BATCH * PAST_LEN
    idx = lambda l, b: (l, 0, b, 0)
    return pl.pallas_call(
        _kvup_cache_kernel,
        grid=(DEPTH, DEC_BATCH),
        in_specs=[
            pl.BlockSpec((None, None, PAST_LEN, KV_RANK), lambda l, b: (b, l, 0, 0)),
            pl.BlockSpec((None, None, PAST_LEN, ROPE_B), lambda l, b: (b, l, 0, 0)),
            pl.BlockSpec((None, KV_RANK, KV_W), lambda l, b: (l, 0, 0)),
            pl.BlockSpec((ROPE_B, QB_PAD), lambda l, b: (0, 0)),
        ],
        out_specs=[pl.BlockSpec((None, H_B, PAST_LEN, LANES), idx), pl.BlockSpec((None, N_PAIR, PAST_LEN, 2 * LANES), idx)],
        out_shape=[jax.ShapeDtypeStruct((DEPTH, H_B, rows, LANES), BF16),
                   jax.ShapeDtypeStruct((DEPTH, N_PAIR, rows, 2 * LANES), BF16)],
        compiler_params=_cparams(2),
        name="kvup_cache",
    )(cache_ckv, cache_kr, w_kv, place32)


def _attn_kernel(*refs, n_seg, c_exp, mask_q):
    qe_ref, qo_ref = refs[0:2]
    k_refs = refs[2:2 + 2 * n_seg]
    v_refs = refs[2 + 2 * n_seg:2 + 3 * n_seg]
    o_ref = refs[-1]
    lo = lax.broadcasted_iota(jnp.int32, o_ref.shape, 1) < HD_A

    def scores(q, ks):
        return [lax.dot_general(q, k[...], (((1,), (1,)), ((), ())), preferred_element_type=F32) for k in ks]

    def attend(ss):
        m = functools.reduce(jnp.maximum, [jnp.max(s, axis=-1, keepdims=True) for s in ss]) * c_exp
        full = functools.reduce(jnp.add, [_dot(jnp.exp2(s * c_exp - m).astype(BF16), v[...]) for s, v in zip(ss, v_refs)])
        return full[:, 0:LANES] / full[:, LANES:LANES + 1]

    qe, qo = qe_ref[...], qo_ref[...]
    if mask_q:
        qe = jnp.where(lo, qe, jnp.zeros_like(qe))
        qo = jnp.where(lo, jnp.zeros_like(qo), qo)
    ss_e = scores(qe, k_refs[0::2])
    ss_o = scores(qo, k_refs[1::2])
    o_ref[...] = jnp.where(lo, attend(ss_e), attend(ss_o)).astype(o_ref.dtype)


def _attention(q, segs, *, n_batch, t_len, tok0, qe_lane, qo_lane, scale, mask_q, name):
    tq = TQ
    nq = t_len // tq
    q_blk0 = tok0 // tq

    def q_map(lane_fn):
        return lambda b, j, i: (q_blk0 + b * nq + i, lane_fn(j))

    def kv_map(f):
        return lambda b, j, i: f(b, j)

    in_specs = [pl.BlockSpec((tq, LANES), q_map(qe_lane)), pl.BlockSpec((tq, LANES), q_map(qo_lane))]
    args = [q, q]
    for k, _, k_blk, _, ke_idx, ko_idx, _ in segs:
        in_specs += [pl.BlockSpec(k_blk, kv_map(ke_idx)), pl.BlockSpec(k_blk, kv_map(ko_idx))]
        args += [k, k]
    for _, v, _, v_blk, _, _, v_idx in segs:
        in_specs.append(pl.BlockSpec(v_blk, kv_map(v_idx)))
        args.append(v)
    return pl.pallas_call(
        functools.partial(_attn_kernel, n_seg=len(segs), c_exp=scale * LOG2E, mask_q=mask_q),
        grid=(n_batch, N_PAIR, nq),
        in_specs=in_specs,
        out_specs=pl.BlockSpec((tq, LANES), lambda b, j, i: (b * nq + i, j)),
        out_shape=jax.ShapeDtypeStruct((n_batch * t_len, N_PAIR * LANES), BF16),
        compiler_params=_cparams(3),
        name=name,
    )(*args)


def _one_minus_exp(x):
    e = jnp.exp(x)
    d = jnp.log(e)
    near = (1.0 - e) * x / jnp.where(d == 0.0, 1.0, d)
    near = jnp.where(d == 0.0, -x, near)
    return jnp.where(x > -0.5, near, 1.0 - e)


def _rglru_kernel(h_ref, wx_ref, wy_ref, cw_ref, cb_ref, wg_ref, bg_ref, lam_ref, h0_ref,
                  oc_ref, st_ref, af_ref, bf_ref, ab_ref, bb_ref, *, t_len):
    h = h_ref[...]
    xr = _dot(h, wx_ref[...])
    rows = lax.broadcasted_iota(jnp.int32, (t_len, 1), 0)
    x_m2 = jnp.where(rows >= 2, pltpu.roll(xr, 2, 0), 0.0)
    x_m1 = jnp.where(rows >= 1, pltpu.roll(xr, 1, 0), 0.0)
    x_p1 = jnp.where(rows < t_len - 1, pltpu.roll(xr, t_len - 1, 0), 0.0)
    u = cw_ref[0:1, :] * x_m2 + cw_ref[1:2, :] * x_m1 + cw_ref[2:3, :] * xr + cw_ref[3:4, :] * x_p1 + cb_ref[...]

    lam = lam_ref[...]
    log_sig = jnp.minimum(lam, 0.0) - jnp.log1p(jnp.exp(-jnp.abs(lam)))
    n_blk = RNN_CB // RNN_BS
    for j in range(n_blk):
        sl = slice(j * RNN_BS, (j + 1) * RNN_BS)
        uj = u[:, sl]
        g = _dot(uj.astype(BF16), wg_ref[j])
        for d, (a_ref, b_ref) in enumerate(((af_ref, bf_ref), (ab_ref, bb_ref))):
            r = jax.nn.sigmoid(g[:, (2 * d) * RNN_BS:(2 * d + 1) * RNN_BS] + bg_ref[2 * d:2 * d + 1, sl])
            i = jax.nn.sigmoid(g[:, (2 * d + 1) * RNN_BS:(2 * d + 2) * RNN_BS] + bg_ref[2 * d + 1:2 * d + 2, sl])
            log_a = RG_C * r * log_sig[d:d + 1, sl]
            a_ref[j] = jnp.exp(log_a)
            b_ref[j] = jnp.sqrt(_one_minus_exp(2.0 * log_a)) * (i * uj)

    n_step = t_len // SUBLANES

    def step(i, carry):
        fwd = pl.ds(i, SUBLANES, stride=n_step)
        bwd = pl.ds(n_step - 1 - i, SUBLANES, stride=n_step)
        out = []
        for j in range(n_blk):
            hf, pf, hb, pb = carry[4 * j:4 * j + 4]
            a = af_ref[j, fwd, :]
            hf = a * hf + bf_ref[j, fwd, :]
            pf = a * pf
            bf_ref[j, fwd, :] = hf
            af_ref[j, fwd, :] = pf
            a = ab_ref[j, bwd, :]
            hb = a * hb + bb_ref[j, bwd, :]
            pb = a * pb
            bb_ref[j, bwd, :] = hb
            ab_ref[j, bwd, :] = pb
            out += [hf, pf, hb, pb]
        return tuple(out)

    zero = jnp.zeros((SUBLANES, RNN_BS), F32)
    one = jnp.ones((SUBLANES, RNN_BS), F32)
    carry = lax.fori_loop(0, n_step, step, (zero, one, zero, one) * n_blk, unroll=4)

    yr = _dot(h, wy_ref[...])
    for j in range(n_blk):
        sl = slice(j * RNN_BS, (j + 1) * RNN_BS)
        hf, pf, hb, pb = carry[4 * j:4 * j + 4]
        entry_f = [h0_ref[0:1, sl]]
        for c in range(SUBLANES):
            entry_f.append(hf[c:c + 1, :] + pf[c:c + 1, :] * entry_f[c])
        entry_b = [h0_ref[1:2, sl]]
        for c in reversed(range(SUBLANES)):
            entry_b.append(hb[c:c + 1, :] + pb[c:c + 1, :] * entry_b[-1])
        st_ref[0:1, sl] = entry_f[SUBLANES]
        st_ref[1:2, sl] = entry_b[SUBLANES]
        for c in range(SUBLANES):
            rs = slice(c * n_step, (c + 1) * n_step)
            hs = (bf_ref[j, rs, :] + af_ref[j, rs, :] * entry_f[c]) + (
                bb_ref[j, rs, :] + ab_ref[j, rs, :] * entry_b[SUBLANES - 1 - c])
            oc_ref[rs, sl] = (jax.nn.gelu(yr[rs, sl]) * hs).astype(BF16)


def _rglru(l, h_all, w_xy, conv_w, conv_b, wg, bg, lam, h0, h0_map, *, n_seq, t_len, tok0, name):
    cb = RNN_CB
    nb = D_RNN // cb
    seq_blk0 = tok0 // t_len
    chan3 = lambda b, n: (l, 0, n)
    return pl.pallas_call(
        functools.partial(_rglru_kernel, t_len=t_len),
        grid=(n_seq, nb),
        in_specs=[
            pl.BlockSpec((t_len, D_MODEL), lambda b, n: (seq_blk0 + b, 0)),
            pl.BlockSpec((None, D_MODEL, cb), chan3),
            pl.BlockSpec((None, D_MODEL, cb), lambda b, n: (l, 0, nb + n)),
            pl.BlockSpec((None, 4, cb), chan3),
            pl.BlockSpec((None, 1, cb), chan3),
            pl.BlockSpec((None, cb // RNN_BS, RNN_BS, 4 * RNN_BS), lambda b, n: (l, n, 0, 0)),
            pl.BlockSpec((None, 4, cb), chan3),
            pl.BlockSpec((None, 2, cb), chan3),
            pl.BlockSpec((None, None, 2, cb), h0_map),
        ],
        out_specs=[
            pl.BlockSpec((t_len, cb), lambda b, n: (b, n)),
            pl.BlockSpec((None, 2, cb), lambda b, n: (b, 0, n)),
        ],
        out_shape=[
            jax.ShapeDtypeStruct((n_seq * t_len, D_RNN), BF16),
            jax.ShapeDtypeStruct((n_seq, 2, D_RNN), F32),
        ],
        scratch_shapes=[pltpu.VMEM((cb // RNN_BS, t_len, RNN_BS), F32)] * 4,
        compiler_params=_cparams(2),
        name=name,
    )(h_all, w_xy, w_xy, conv_w, conv_b, wg, bg, lam, h0)


def _merge_kernel(oac_ref, oal_ref, obc_ref, obl_ref, occ_ref, ocl_ref, h_ref, x_ref, mod_ref, gpost_ref, gpre2_ref,
                  woa_ref, wob_ref, woc_ref, wgl_ref, wout_ref, x1_ref, h2_ref):
    h = h_ref[...]
    is_ctx = pl.program_id(0) < N_CTX // TM_MERGE

    def gate(k):
        return jax.nn.sigmoid(_dot(h, wgl_ref[:, k * D_MODEL:(k + 1) * D_MODEL]))

    def branch(c_ref, l_ref, w_ref):
        return _dot(jnp.where(is_ctx, c_ref[...], l_ref[...]), w_ref[...])

    merged = gate(0) * branch(oac_ref, oal_ref, woa_ref)
    merged = merged + gate(1) * branch(obc_ref, obl_ref, wob_ref)
    merged = merged + gate(2) * branch(occ_ref, ocl_ref, woc_ref)
    out = _dot(merged.astype(BF16), wout_ref[...])
    gt1 = mod_ref[:, 2 * D_MODEL:3 * D_MODEL]
    sh2 = mod_ref[:, 3 * D_MODEL:4 * D_MODEL]
    sc2 = mod_ref[:, 4 * D_MODEL:5 * D_MODEL]
    x1 = x_ref[...] + gt1 * _rms(out, gpost_ref[...])
    x1_ref[...] = x1
    h2_ref[...] = (_rms(x1, gpre2_ref[...]) * (1.0 + sc2) + sh2).astype(BF16)


def _merge(l, oa, ob, oc, h, x, mods, gpost, gpre2, w_oa, w_ob, w_oc, w_gl, w_out):
    tm = TM_MERGE
    nct = N_CTX // tm
    row = lambda i: (i, 0)
    ctx = lambda i: (jnp.minimum(i, nct - 1), 0)
    lat = lambda i: (jnp.maximum(i - nct, 0), 0)
    layer = lambda i: (l, 0, 0)

    def pair(width):
        return [pl.BlockSpec((tm, width), ctx), pl.BlockSpec((tm, width), lat)]

    return pl.pallas_call(
        _merge_kernel,
        grid=(N_TOK // tm,),
        in_specs=pair(QA_W) + pair(OB_W) + pair(D_RNN) + [
            pl.BlockSpec((tm, D_MODEL), row),
            pl.BlockSpec((tm, D_MODEL), row),
            pl.BlockSpec((None, 1, 6 * D_MODEL), lambda i: (_mod_row(l, i, tm), 0, 0)),
            pl.BlockSpec((None, 1, D_MODEL), layer),
            pl.BlockSpec((None, 1, D_MODEL), layer),
            pl.BlockSpec((None, QA_W, D_MODEL), layer),
            pl.BlockSpec((None, OB_W, D_MODEL), layer),
            pl.BlockSpec((None, D_RNN, D_MODEL), layer),
            pl.BlockSpec((None, D_MODEL, 3 * D_MODEL), layer),
            pl.BlockSpec((None, D_MODEL, D_MODEL), layer),
        ],
        out_specs=[pl.BlockSpec((tm, D_MODEL), row), pl.BlockSpec((tm, D_MODEL), row)],
        out_shape=[
            jax.ShapeDtypeStruct((N_TOK, D_MODEL), F32),
            jax.ShapeDtypeStruct((N_TOK, D_MODEL), BF16),
        ],
        compiler_params=_cparams(1),
        name="merge",
    )(*oa, *ob, *oc, h, x, mods, gpost, gpre2, w_oa, w_ob, w_oc, w_gl, w_out)


def _ffn_kernel(hp_ref, hm_ref, hn_ref, x1_ref, mod_ref, gpost_ref, wv_ref, wg_ref, cwv_ref, cwg_ref,
                cbv_ref, cbg_ref, wd_ref, o_ref, hext_ref, acc_ref, *, tile0):
    i = pl.program_id(0) + tile0
    c = pl.program_id(1)
    tm = TM_FFN

    @pl.when(c == 0)
    def _():
        hext_ref[0:HALO, :] = hp_ref[...]
        hext_ref[HALO:HALO + tm, :] = hm_ref[...]
        hext_ref[HALO + tm:, :] = hn_ref[...]
        acc_ref[...] = jnp.zeros_like(acc_ref)

    r = i * tm + lax.broadcasted_iota(jnp.int32, (tm, 1), 0)
    t = jnp.where(r < N_CTX, r & (SEQ - 1), r & (DEC_SEQ - 1))
    t_last = jnp.where(r < N_CTX, SEQ - 1, DEC_SEQ - 1)
    first = t == 0
    last = t == t_last
    hext = hext_ref[...]
    n_ext = tm + 2 * HALO

    def conv(w_ref, cw_ref, cb_ref):
        up = _dot(hext, w_ref[...])
        prev = jnp.where(first, 0.0, pltpu.roll(up, 1, 0)[HALO:HALO + tm])
        nxt = jnp.where(last, 0.0, pltpu.roll(up, n_ext - 1, 0)[HALO:HALO + tm])
        return cw_ref[0:1, :] * prev + cw_ref[1:2, :] * up[HALO:HALO + tm] + cw_ref[2:3, :] * nxt + cb_ref[...]

    val = conv(wv_ref, cwv_ref, cbv_ref)
    gat = conv(wg_ref, cwg_ref, cbg_ref)
    acc_ref[...] += _dot((jax.nn.gelu(gat) * val).astype(BF16), wd_ref[...])

    @pl.when(c == pl.num_programs(1) - 1)
    def _():
        gt2 = mod_ref[:, 5 * D_MODEL:6 * D_MODEL]
        o_ref[...] = x1_ref[...] + gt2 * _rms(acc_ref[...], gpost_ref[...])


def _ffn(l, h2, x1, mods, gpost, w_up, conv_w, conv_b, w_down, *, tok0, n_tok, name):
    tm, tf = TM_FFN, TF_FFN
    nc = D_FF // tf
    per = tm // HALO
    n_halo_blocks = N_TOK // HALO
    t0 = tok0 // tm
    row = lambda i, c: (t0 + i, 0)
    layer = lambda i, c: (l, 0, 0)
    return pl.pallas_call(
        functools.partial(_ffn_kernel, tile0=t0),
        grid=(n_tok // tm, nc),
        in_specs=[
            pl.BlockSpec((HALO, D_MODEL), lambda i, c: (jnp.maximum((t0 + i) * per - 1, 0), 0)),
            pl.BlockSpec((tm, D_MODEL), row),
            pl.BlockSpec((HALO, D_MODEL), lambda i, c: (jnp.minimum((t0 + i + 1) * per, n_halo_blocks - 1), 0)),
            pl.BlockSpec((tm, D_MODEL), row),
            pl.BlockSpec((None, 1, 6 * D_MODEL), lambda i, c: (_mod_row(l, t0 + i, tm), 0, 0)),
            pl.BlockSpec((None, 1, D_MODEL), layer),
            pl.BlockSpec((None, D_MODEL, tf), lambda i, c: (l, 0, c)),
            pl.BlockSpec((None, D_MODEL, tf), lambda i, c: (l, 0, nc + c)),
            pl.BlockSpec((None, 3, tf), lambda i, c: (l, 0, c)),
            pl.BlockSpec((None, 3, tf), lambda i, c: (l, 0, nc + c)),
            pl.BlockSpec((None, 1, tf), lambda i, c: (l, 0, c)),
            pl.BlockSpec((None, 1, tf), lambda i, c: (l, 0, nc + c)),
            pl.BlockSpec((None, tf, D_MODEL), lambda i, c: (l, c, 0)),
        ],
        out_specs=pl.BlockSpec((tm, D_MODEL), lambda i, c: (i, 0)),
        out_shape=jax.ShapeDtypeStruct((n_tok, D_MODEL), F32),
        scratch_shapes=[pltpu.VMEM((tm + 2 * HALO, D_MODEL), BF16), pltpu.VMEM((tm, D_MODEL), F32)],
        compiler_params=_cparams(2),
        name=name,
    )(h2, h2, h2, x1, mods, gpost, w_up, w_up, conv_w, conv_w, conv_b, conv_b, w_down)


def _rope_tables():
    t = jnp.arange(DEC_SEQ)
    row = (t // GRID_W).astype(F32)[:, None]
    col = (t % GRID_W).astype(F32)[:, None]

    def parts(dim):
        n = dim // 4
        inv = ROPE_THETA ** (-jnp.arange(n, dtype=F32) / n)
        ar, ac = row * inv, col * inv
        z = jnp.zeros_like(ar)
        cos = jnp.concatenate([jnp.cos(ar), jnp.cos(ar), jnp.cos(ac), jnp.cos(ac)], axis=-1)
        s_up = jnp.concatenate([-jnp.sin(ar), z, -jnp.sin(ac), z], axis=-1)
        s_dn = jnp.concatenate([z, jnp.sin(ar), z, jnp.sin(ac)], axis=-1)
        return cos, s_up, s_dn

    a = [jnp.tile(p, (1, LANES // HD_A)) for p in parts(HD_A)]
    pad = lambda p, fill: jnp.concatenate(
        [fill * jnp.ones((DEC_SEQ, KR_LANE), F32), p, fill * jnp.ones((DEC_SEQ, LANES - KR_LANE - ROPE_B), F32)], axis=-1)
    cb, sbu, sbd = parts(ROPE_B)
    b = [pad(cb, 1.0), pad(sbu, 0.0), pad(sbd, 0.0)]
    lat = jnp.concatenate(a + b, axis=-1)
    ident_blk = jnp.concatenate([jnp.ones((TM_IN, LANES), F32), jnp.zeros((TM_IN, 2 * LANES), F32)], axis=-1)
    ident = jnp.concatenate([ident_blk, ident_blk], axis=-1)
    return jnp.concatenate([ident, lat], axis=0)


def _pack_w_att(w_in):
    qb = w_in[:, :, 768:1536].reshape(DEPTH, D_MODEL, H_B, NOPE_B + ROPE_B)
    qb = jnp.pad(qb, ((0, 0), (0, 0), (0, 0), (0, LANES - NOPE_B - ROPE_B))).reshape(DEPTH, D_MODEL, QB_PAD)
    kr = jnp.pad(w_in[:, :, 1792:1824], ((0, 0), (0, 0), (KR_LANE, LANES - KR_LANE - ROPE_B)))
    return jnp.concatenate([w_in[:, :, 0:768], qb, w_in[:, :, 1536:1792], kr], axis=-1).astype(BF16)


def _pack_w_kv(w_uk, w_uv):
    uk = w_uk.reshape(DEPTH, KV_RANK, H_B, NOPE_B)
    k_part = jnp.pad(uk, ((0, 0), (0, 0), (0, 0), (0, LANES - NOPE_B))).reshape(DEPTH, KV_RANK, QB_PAD)
    return jnp.concatenate([k_part, w_uv], axis=-1).astype(BF16)


def _cache_dup_heads(cache, with_ones):
    x = jnp.transpose(cache, (1, 3, 0, 2, 4)).astype(BF16)
    parts = [x, x]
    if with_ones:
        parts.append(jnp.broadcast_to((jnp.arange(LANES) == 0).astype(BF16), x.shape[:-1] + (LANES,)))
    return jnp.concatenate(parts, axis=-1).reshape(DEPTH, KV_A, DEC_BATCH * PAST_LEN, -1)


def kernel(x_prompt, x_sample, c, cache_gqa_k, cache_gqa_v, cache_mla_ckv, cache_mla_krope, state_rglru_fwd, state_rglru_bwd, c_ctx, w_ada, b_ada, g_pre_mix, g_post_mix, g_pre_ffn, g_post_ffn, w_in, g_qa, g_ka, g_ckv, w_uk, w_uv, conv_rnn_w, conv_rnn_b, w_rg, b_rg, w_ig, b_ig, lam, w_oa, w_ob, w_oc, w_out, w_up, conv_ffn_w, conv_ffn_b, w_down):
    x = jnp.concatenate([x_prompt.reshape(N_CTX, D_MODEL), x_sample.reshape(N_LAT, D_MODEL)], axis=0)
    cvec = jnp.concatenate([c_ctx[None, :], c, jnp.zeros((N_MOD_ROWS - 1 - DEC_BATCH, D_MODEL), F32)], axis=0)
    mods = _modulation(cvec, w_ada, b_ada).reshape(DEPTH * N_MOD_ROWS, 1, 6 * D_MODEL)

    vec = lambda g: g.reshape(DEPTH, 1, -1)
    tab = _rope_tables()
    seg = jnp.arange(QA_W) // HD_A
    bd = jnp.where(seg[:, None] == seg[None, :], 1.0 / HD_A, 0.0).astype(BF16)
    lane = jnp.arange(QB_PAD) % LANES
    place32 = (lane[None, :] == KR_LANE + jnp.arange(ROPE_B)[:, None]).astype(BF16)
    place = jnp.pad(place32, ((KR_LANE, LANES - KR_LANE - ROPE_B), (0, 0)))
    w_att = _pack_w_att(w_in)
    w_xy = w_in[:, :, 1824:3872].astype(BF16)
    w_gl = w_in[:, :, 3872:6944].astype(BF16)
    w_kv = _pack_w_kv(w_uk, w_uv)
    gqa_t = jnp.tile(g_qa, (1, H_A)).reshape(DEPTH, 1, QA_W)
    gka_t = jnp.tile(g_ka, (1, KV_A)).reshape(DEPTH, 1, KA_W)
    wg = jnp.concatenate([w_rg[:, 0], w_ig[:, 0], w_rg[:, 1], w_ig[:, 1]], axis=-1).astype(BF16)
    bg = jnp.stack([b_rg[:, 0], b_ig[:, 0], b_rg[:, 1], b_ig[:, 1]], axis=1)
    h0_lat = jnp.stack([state_rglru_fwd, state_rglru_bwd], axis=2)
    h0_zero = jnp.zeros((1, 1, 2, D_RNN), F32)
    w_oa_b, w_ob_b, w_oc_b, w_out_b = (w.astype(BF16) for w in (w_oa, w_ob, w_oc, w_out))
    w_up_b, w_down_b = w_up.astype(BF16), w_down.astype(BF16)
    conv_ffn_b3 = vec(conv_ffn_b)

    kc_a = _cache_dup_heads(cache_gqa_k, False)
    vc_a = _cache_dup_heads(cache_gqa_v, True)
    kc_b, vc_b = _kvup_cache(cache_mla_ckv, cache_mla_krope, w_kv, place32)

    lat_blk0 = N_CTX // DEC_SEQ
    gqa_k = gqa_v = lambda j: j // 2
    mla_ke = lambda j: 2 * j
    mla_ko = lambda j: 2 * j + 1
    mla_v = lambda j: j

    def new_seg(k, v, rows, blk0, fe, fo, fv):
        at = lambda f: lambda b, j: (f(j), blk0 + b, 0)
        return (k, v, (None, rows, LANES), (None, rows, 2 * LANES), at(fe), at(fo), at(fv))

    new_k, new_v, new_ckv, new_kr, new_fwd, new_bwd = [], [], [], [], [], []
    y = z = None
    for l in range(DEPTH):
        qa, qb, ka, va, kdup, vdup, ckv, kr, kb, vb, h = _inproj(
            l, x, mods, vec(g_pre_mix), tab, w_att, bd, gqa_t, gka_t, vec(g_ckv), w_kv, place)

        def cache_seg(k, v, fe, fo, fv):
            at = lambda f: lambda b, j: (l, f(j), b, 0)
            return (k, v, (None, None, PAST_LEN, LANES), (None, None, PAST_LEN, 2 * LANES), at(fe), at(fo), at(fv))

        gqa_idx = (gqa_k, gqa_k, gqa_v)
        gqa = dict(qe_lane=lambda j: j, qo_lane=lambda j: j, scale=1.0, mask_q=True)
        oa_ctx = _attention(qa, [new_seg(kdup, vdup, SEQ, 0, *gqa_idx)],
                            n_batch=BATCH, t_len=SEQ, tok0=0, name="gqa_ctx", **gqa)
        oa_lat = _attention(qa, [cache_seg(kc_a, vc_a, *gqa_idx), new_seg(kdup, vdup, DEC_SEQ, lat_blk0, *gqa_idx)],
                            n_batch=DEC_BATCH, t_len=DEC_SEQ, tok0=N_CTX, name="gqa_lat", **gqa)
        mla_idx = (mla_ke, mla_ko, mla_v)
        mla = dict(qe_lane=mla_ke, qo_lane=mla_ko, scale=(NOPE_B + ROPE_B) ** -0.5, mask_q=False)
        ob_ctx = _attention(qb, [new_seg(kb, vb, SEQ, 0, *mla_idx)],
                            n_batch=BATCH, t_len=SEQ, tok0=0, name="mla_ctx", **mla)
        ob_lat = _attention(qb, [cache_seg(kc_b, vc_b, *mla_idx), new_seg(kb, vb, DEC_SEQ, lat_blk0, *mla_idx)],
                            n_batch=DEC_BATCH, t_len=DEC_SEQ, tok0=N_CTX, name="mla_lat", **mla)

        rnn_args = (w_xy, conv_rnn_w, vec(conv_rnn_b), wg, bg, lam)
        oc_ctx, st_ctx = _rglru(l, h, *rnn_args, h0_zero, lambda b, n: (0, 0, 0, n),
                                n_seq=BATCH, t_len=SEQ, tok0=0, name="rglru_ctx")
        oc_lat, _ = _rglru(l, h, *rnn_args, h0_lat, lambda b, n: (b, l, 0, n),
                           n_seq=DEC_BATCH, t_len=DEC_SEQ, tok0=N_CTX, name="rglru_lat")

        x1, h2 = _merge(l, (oa_ctx, oa_lat), (ob_ctx, ob_lat), (oc_ctx, oc_lat), h, x, mods,
                        vec(g_post_mix), vec(g_pre_ffn), w_oa_b, w_ob_b, w_oc_b, w_gl, w_out_b)
        ffn_args = (l, h2, x1, mods, vec(g_post_ffn), w_up_b, conv_ffn_w, conv_ffn_b3, w_down_b)
        if l < DEPTH - 1:
            x = _ffn(*ffn_args, tok0=0, n_tok=N_TOK, name="ffn")
        else:
            y = _ffn(*ffn_args, tok0=0, n_tok=N_CTX, name="ffn_ctx")
            z = _ffn(*ffn_args, tok0=N_CTX, n_tok=N_LAT, name="ffn_lat")

        new_k.append(ka[:N_CTX].reshape(BATCH, SEQ, KV_A, HD_A))
        new_v.append(va[:N_CTX].reshape(BATCH, SEQ, KV_A, HD_A))
        new_ckv.append(ckv[:N_CTX].reshape(BATCH, SEQ, KV_RANK))
        new_kr.append(kr[:N_CTX].reshape(BATCH, SEQ, ROPE_B))
        new_fwd.append(st_ctx[:, 0])
        new_bwd.append(st_ctx[:, 1])

    stack = lambda xs: jnp.stack(xs, axis=1)
    return (y.reshape(BATCH, SEQ, D_MODEL), z.reshape(DEC_BATCH, DEC_SEQ, D_MODEL),
            stack(new_k), stack(new_v), stack(new_ckv), stack(new_kr), stack(new_fwd), stack(new_bwd))
```

```python
import functools
import math

import jax
import jax.numpy as jnp
from jax import lax
from jax.experimental import pallas as pl
from jax.experimental.pallas import tpu as pltpu

F32 = jnp.float32
BF16 = jnp.bfloat16

D_MODEL = 1024
BATCH = 16
SEQ = 256
DEPTH = 2
DEC_BATCH = 4
DEC_SEQ = 2048
PAST_LEN = 512
GRID_W = 64
H_A = 8
KV_A = 2
HD_A = 64
H_B = 8
NOPE_B = 64
ROPE_B = 32
VD_B = 64
KV_RANK = 256
D_RNN = 1024
RNN_BLOCKS = 8
RNN_BS = D_RNN // RNN_BLOCKS
RG_C = 8.0
D_FF = 2816
ROPE_THETA = 10000.0
EPS = 1e-6
QA_W = H_A * HD_A
KA_W = KV_A * HD_A
QB_W = H_B * (NOPE_B + ROPE_B)
OB_W = H_B * VD_B

LANES = 128
SUBLANES = 8
N_CTX = BATCH * SEQ
N_LAT = DEC_BATCH * DEC_SEQ
N_TOK = N_CTX + N_LAT
N_MOD_ROWS = 8
QB_PAD = H_B * LANES
KR_LANE = NOPE_B
N_PAIR = 4
OFF_QA, OFF_KA, OFF_VA, OFF_QB, OFF_CKV, OFF_KR = 0, 512, 640, 768, 1792, 2048
ATT_W = OFF_KR + LANES
ATT_BLK = 2304
OFF_XR = ATT_BLK
OFF_GL = 6144
PROJ_W = OFF_GL + 3 * D_MODEL
KV_W = QB_PAD + OB_W
TAB_W = 6 * LANES
LOG2E = math.log2(math.e)

TM_IN = 256
TQ = 512
RNN_CB = 256
TM_MERGE = 512
TM_FFN = 512
TF_FFN = D_FF // 2
HALO = 16
VMEM_LIMIT = 56 * 1024 * 1024


def _cparams(n_axes):
    return pltpu.CompilerParams(dimension_semantics=("arbitrary",) * n_axes, vmem_limit_bytes=VMEM_LIMIT)


def _dot(a, b):
    return jnp.dot(a, b, preferred_element_type=F32)


def _rms(x, g):
    return x * lax.rsqrt(jnp.mean(x * x, axis=-1, keepdims=True) + EPS) * g


def _mod_row(l, i, tm):
    n_ctx_tiles = N_CTX // tm
    return l * N_MOD_ROWS + jnp.where(i < n_ctx_tiles, 0, 1 + (i - n_ctx_tiles) // (DEC_SEQ // tm))


def _mod_kernel(c_ref, w_ref, b_ref, o_ref):
    c = c_ref[...]
    s = (c * jax.nn.sigmoid(c)).astype(BF16)
    o_ref[...] = _dot(s, w_ref[...].astype(BF16)) + b_ref[...]


def _modulation(cvec, w_ada, b_ada):
    tn = 1536
    return pl.pallas_call(
        _mod_kernel,
        grid=(DEPTH, 6 * D_MODEL // tn),
        in_specs=[
            pl.BlockSpec((N_MOD_ROWS, D_MODEL), lambda l, n: (0, 0)),
            pl.BlockSpec((None, D_MODEL, tn), lambda l, n: (l, 0, n)),
            pl.BlockSpec((None, 1, tn), lambda l, n: (l, 0, n)),
        ],
        out_specs=pl.BlockSpec((None, N_MOD_ROWS, tn), lambda l, n: (l, 0, n)),
        out_shape=jax.ShapeDtypeStruct((DEPTH, N_MOD_ROWS, 6 * D_MODEL), F32),
        compiler_params=_cparams(2),
        name="modulation",
    )(cvec, w_ada, b_ada.reshape(DEPTH, 1, 6 * D_MODEL))


def _seg_mean(x2, bd):
    hi = x2.astype(BF16)
    lo = (x2 - hi.astype(F32)).astype(BF16)
    return _dot(hi, bd) + _dot(lo, bd)


def _rope(x, cos, sin_up, sin_dn, shift):
    w = x.shape[-1]
    return x * cos + pltpu.roll(x, w - shift, 1) * sin_up + pltpu.roll(x, shift, 1) * sin_dn


def _dup_heads(x):
    lo = lax.broadcasted_iota(jnp.int32, x.shape, 1) < HD_A
    sw = pltpu.roll(x, HD_A, 1)
    return jnp.where(lo, x, sw).astype(BF16), jnp.where(lo, sw, x).astype(BF16)


def _ones_column(rows):
    return jnp.where(lax.broadcasted_iota(jnp.int32, (rows, LANES), 1) == 0, 1.0, 0.0).astype(BF16)


def _store_mla_kv(y, kr_all, ones, k_ref, v_ref):
    for hd in range(H_B):
        sl = slice(hd * LANES, (hd + 1) * LANES)
        k_ref[hd] = (y[:, sl] + kr_all[:, sl]).astype(BF16)
    for j in range(N_PAIR):
        v_ref[j, :, 0:LANES] = y[:, QB_PAD + j * LANES:QB_PAD + (j + 1) * LANES].astype(BF16)
        v_ref[j, :, LANES:2 * LANES] = ones


def _inproj_kernel(x_ref, mod_ref, gpre_ref, tab_ref, w_ref, bd_ref, gqa_ref, gka_ref, gckv_ref, wkv_ref, place_ref,
                   qa_ref, qb_ref, ka_ref, va_ref, kdup_ref, vdup_ref, ckv_ref, kr_ref, kb_ref, vb_ref, h_ref):
    x = x_ref[...]
    sh1 = mod_ref[:, 0:D_MODEL]
    sc1 = mod_ref[:, D_MODEL:2 * D_MODEL]
    h = (_rms(x, gpre_ref[...]) * (1.0 + sc1) + sh1).astype(BF16)
    h_ref[...] = h
    y = _dot(h, w_ref[:, 0:ATT_W])

    cos_a, sa_up, sa_dn = tab_ref[:, 0:128], tab_ref[:, 128:256], tab_ref[:, 256:384]
    cos_b, sb_up, sb_dn = tab_ref[:, 384:512], tab_ref[:, 512:640], tab_ref[:, 640:768]

    q = y[:, OFF_QA:OFF_QA + QA_W]
    q = q * lax.rsqrt(_seg_mean(q * q, bd_ref[...]) + EPS) * gqa_ref[...]
    rep = QA_W // LANES
    q = _rope(q, jnp.tile(cos_a, (1, rep)), jnp.tile(sa_up, (1, rep)), jnp.tile(sa_dn, (1, rep)), HD_A // 4)
    qa_ref[...] = (q * (HD_A ** -0.5)).astype(BF16)

    k = y[:, OFF_KA:OFF_KA + KA_W]
    k = k * lax.rsqrt(_seg_mean(k * k, bd_ref[0:KA_W, 0:KA_W]) + EPS) * gka_ref[...]
    k = _rope(k, cos_a, sa_up, sa_dn, HD_A // 4)
    ka_ref[...] = k
    v = y[:, OFF_VA:OFF_VA + KA_W]
    va_ref[...] = v
    ones = _ones_column(k.shape[0])
    for n, (kd, vd) in enumerate(zip(_dup_heads(k), _dup_heads(v))):
        kdup_ref[n] = kd
        vdup_ref[n, :, 0:LANES] = vd
        vdup_ref[n, :, LANES:2 * LANES] = ones

    qb = y[:, OFF_QB:OFF_QB + QB_PAD]
    qb = _rope(qb, jnp.tile(cos_b, (1, H_B)), jnp.tile(sb_up, (1, H_B)), jnp.tile(sb_dn, (1, H_B)), ROPE_B // 4)
    qb_ref[...] = qb.astype(BF16)

    ckv = _rms(y[:, OFF_CKV:OFF_CKV + KV_RANK], gckv_ref[...])
    ckv_ref[...] = ckv
    kr = _rope(y[:, OFF_KR:OFF_KR + LANES], cos_b, sb_up, sb_dn, ROPE_B // 4)
    kr_ref[...] = kr[:, KR_LANE:KR_LANE + ROPE_B]

    y2 = _dot(ckv.astype(BF16), wkv_ref[...])
    kr_all = _dot(kr.astype(BF16), place_ref[...])
    _store_mla_kv(y2, kr_all, ones, kb_ref, vb_ref)


def _inproj(l, x, mods, gpre, tab, w_att, bd, gqa_t, gka_t, gckv, w_kv, place):
    tm = TM_IN
    n_ctx_tiles = N_CTX // tm
    lat_tiles = DEC_SEQ // tm

    def tab_idx(i):
        return (jnp.where(i < n_ctx_tiles, 0, 1 + (i - n_ctx_tiles) % lat_tiles), 0)

    row = lambda i: (i, 0)
    row3 = lambda i: (0, i, 0)
    const = lambda i: (0, 0)
    layer = lambda i: (l, 0, 0)
    return pl.pallas_call(
        _inproj_kernel,
        grid=(N_TOK // tm,),
        in_specs=[
            pl.BlockSpec((tm, D_MODEL), row),
            pl.BlockSpec((None, 1, 6 * D_MODEL), lambda i: (_mod_row(l, i, tm), 0, 0)),
            pl.BlockSpec((None, 1, D_MODEL), layer),
            pl.BlockSpec((tm, TAB_W), tab_idx),
            pl.BlockSpec((None, D_MODEL, ATT_BLK), layer),
            pl.BlockSpec((QA_W, QA_W), const),
            pl.BlockSpec((None, 1, QA_W), layer),
            pl.BlockSpec((None, 1, KA_W), layer),
            pl.BlockSpec((None, 1, KV_RANK), layer),
            pl.BlockSpec((None, KV_RANK, KV_W), layer),
            pl.BlockSpec((LANES, QB_PAD), const),
        ],
        out_specs=[
            pl.BlockSpec((tm, QA_W), row),
            pl.BlockSpec((tm, QB_PAD), row),
            pl.BlockSpec((tm, KA_W), row),
            pl.BlockSpec((tm, KA_W), row),
            pl.BlockSpec((KV_A, tm, LANES), row3),
            pl.BlockSpec((KV_A, tm, 2 * LANES), row3),
            pl.BlockSpec((tm, KV_RANK), row),
            pl.BlockSpec((tm, ROPE_B), row),
            pl.BlockSpec((H_B, tm, LANES), row3),
            pl.BlockSpec((N_PAIR, tm, 2 * LANES), row3),
            pl.BlockSpec((tm, D_MODEL), row),
        ],
        out_shape=[
            jax.ShapeDtypeStruct((N_TOK, QA_W), BF16),
            jax.ShapeDtypeStruct((N_TOK, QB_PAD), BF16),
            jax.ShapeDtypeStruct((N_TOK, KA_W), F32),
            jax.ShapeDtypeStruct((N_TOK, KA_W), F32),
            jax.ShapeDtypeStruct((KV_A, N_TOK, LANES), BF16),
            jax.ShapeDtypeStruct((KV_A, N_TOK, 2 * LANES), BF16),
            jax.ShapeDtypeStruct((N_TOK, KV_RANK), F32),
            jax.ShapeDtypeStruct((N_TOK, ROPE_B), F32),
            jax.ShapeDtypeStruct((H_B, N_TOK, LANES), BF16),
            jax.ShapeDtypeStruct((N_PAIR, N_TOK, 2 * LANES), BF16),
            jax.ShapeDtypeStruct((N_TOK, D_MODEL), BF16),
        ],
        compiler_params=_cparams(1),
        name="inproj",
    )(x, mods, gpre, tab, w_att, bd, gqa_t, gka_t, gckv, w_kv, place)


def _kvup_cache_kernel(c_ref, r_ref, w_ref, place_ref, k_ref, v_ref):
    y = _dot(c_ref[...].astype(BF16), w_ref[...])
    kr_all = _dot(r_ref[...].astype(BF16), place_ref[...])
    _store_mla_kv(y, kr_all, _ones_column(y.shape[0]), k_ref, v_ref)


def _kvup_cache(cache_ckv, cache_kr, w_kv, place32):
    rows = DEC_BATCH * PAST_LEN
    idx = lambda l, b: (l, 0, b, 0)
    return pl.pallas_call(
        _kvup_cache_kernel,
        grid=(DEPTH, DEC_BATCH),
        in_specs=[
            pl.BlockSpec((None, None, PAST_LEN, KV_RANK), lambda l, b: (b, l, 0, 0)),
            pl.BlockSpec((None, None, PAST_LEN, ROPE_B), lambda l, b: (b, l, 0, 0)),
            pl.BlockSpec((None, KV_RANK, KV_W), lambda l, b: (l, 0, 0)),
            pl.BlockSpec((ROPE_B, QB_PAD), lambda l, b: (0, 0)),
        ],
        out_specs=[pl.BlockSpec((None, H_B, PAST_LEN, LANES), idx), pl.BlockSpec((None, N_PAIR, PAST_LEN, 2 * LANES), idx)],
        out_shape=[jax.ShapeDtypeStruct((DEPTH, H_B, rows, LANES), BF16),
                   jax.ShapeDtypeStruct((DEPTH, N_PAIR, rows, 2 * LANES), BF16)],
        compiler_params=_cparams(2),
        name="kvup_cache",
    )(cache_ckv, cache_kr, w_kv, place32)


def _attn_kernel(*refs, n_seg, c_exp, mask_q):
    qe_ref, qo_ref = refs[0:2]
    k_refs = refs[2:2 + 2 * n_seg]
    v_refs = refs[2 + 2 * n_seg:2 + 3 * n_seg]
    o_ref = refs[-1]
    lo = lax.broadcasted_iota(jnp.int32, o_ref.shape, 1) < HD_A

    def scores(q, ks):
        return [lax.dot_general(q, k[...], (((1,), (1,)), ((), ())), preferred_element_type=F32) for k in ks]

    def attend(ss):
        m = functools.reduce(jnp.maximum, [jnp.max(s, axis=-1, keepdims=True) for s in ss]) * c_exp
        full = functools.reduce(jnp.add, [_dot(jnp.exp2(s * c_exp - m).astype(BF16), v[...]) for s, v in zip(ss, v_refs)])
        return full[:, 0:LANES] / full[:, LANES:LANES + 1]

    qe, qo = qe_ref[...], qo_ref[...]
    if mask_q:
        qe = jnp.where(lo, qe, jnp.zeros_like(qe))
        qo = jnp.where(lo, jnp.zeros_like(qo), qo)
    ss_e = scores(qe, k_refs[0::2])
    ss_o = scores(qo, k_refs[1::2])
    o_ref[...] = jnp.where(lo, attend(ss_e), attend(ss_o)).astype(o_ref.dtype)


def _attention(q, segs, *, n_batch, t_len, tok0, qe_lane, qo_lane, scale, mask_q, name):
    tq = min(TQ, t_len)
    nq = t_len // tq
    q_blk0 = tok0 // tq

    def q_map(lane_fn):
        return lambda b, j, i: (q_blk0 + b * nq + i, lane_fn(j))

    def kv_map(f):
        return lambda b, j, i: f(b, j)

    in_specs = [pl.BlockSpec((tq, LANES), q_map(qe_lane)), pl.BlockSpec((tq, LANES), q_map(qo_lane))]
    args = [q, q]
    for k, _, k_blk, _, ke_idx, ko_idx, _ in segs:
        in_specs += [pl.BlockSpec(k_blk, kv_map(ke_idx)), pl.BlockSpec(k_blk, kv_map(ko_idx))]
        args += [k, k]
    for _, v, _, v_blk, _, _, v_idx in segs:
        in_specs.append(pl.BlockSpec(v_blk, kv_map(v_idx)))
        args.append(v)
    return pl.pallas_call(
        functools.partial(_attn_kernel, n_seg=len(segs), c_exp=scale * LOG2E, mask_q=mask_q),
        grid=(n_batch, N_PAIR, nq),
        in_specs=in_specs,
        out_specs=pl.BlockSpec((tq, LANES), lambda b, j, i: (b * nq + i, j)),
        out_shape=jax.ShapeDtypeStruct((n_batch * t_len, N_PAIR * LANES), BF16),
        compiler_params=_cparams(3),
        name=name,
    )(*args)


def _rglru_kernel(h_ref, wx_ref, wy_ref, cw_ref, cb_ref, wg_ref, bg_ref, lam_ref, h0_ref,
                  oc_ref, st_ref, af_ref, bf_ref, ab_ref, bb_ref, *, t_len):
    h = h_ref[...]
    xr = _dot(h, wx_ref[...])
    rows = lax.broadcasted_iota(jnp.int32, (t_len, 1), 0)
    x_m2 = jnp.where(rows >= 2, pltpu.roll(xr, 2, 0), 0.0)
    x_m1 = jnp.where(rows >= 1, pltpu.roll(xr, 1, 0), 0.0)
    x_p1 = jnp.where(rows < t_len - 1, pltpu.roll(xr, t_len - 1, 0), 0.0)
    u = cw_ref[0:1, :] * x_m2 + cw_ref[1:2, :] * x_m1 + cw_ref[2:3, :] * xr + cw_ref[3:4, :] * x_p1 + cb_ref[...]

    lam = lam_ref[...]
    half_c = (0.5 * RG_C) * (jnp.minimum(lam, 0.0) - jnp.log1p(jnp.exp(-jnp.abs(lam))))
    n_blk = RNN_CB // RNN_BS
    for j in range(n_blk):
        sl = slice(j * RNN_BS, (j + 1) * RNN_BS)
        uj = u[:, sl]
        g = _dot(uj.astype(BF16), wg_ref[j])
        for d, (a_ref, b_ref) in enumerate(((af_ref, bf_ref), (ab_ref, bb_ref))):
            tr = jnp.tanh(g[:, (2 * d) * RNN_BS:(2 * d + 1) * RNN_BS] + bg_ref[2 * d:2 * d + 1, sl])
            ti = jnp.tanh(g[:, (2 * d + 1) * RNN_BS:(2 * d + 2) * RNN_BS] + bg_ref[2 * d + 1:2 * d + 2, sl])
            c = half_c[d:d + 1, sl]
            a = jnp.exp(c * tr + c)
            a_ref[j] = a
            om = 1.0 - a * a
            root = jnp.where(om > 0.0, om * lax.rsqrt(om), 0.0)
            b_ref[j] = root * ((0.5 * ti + 0.5) * uj)

    row = lax.broadcasted_iota(jnp.int32, (SUBLANES, RNN_BS), 0)
    shifts = [s for s in (1, 2, 4) if s < SUBLANES]

    def tile_scan(a, b, entry, reverse):
        for s in shifts:
            if reverse:
                live = row < SUBLANES - s
                a_sh, b_sh = pltpu.roll(a, SUBLANES - s, 0), pltpu.roll(b, SUBLANES - s, 0)
            else:
                live = row >= s
                a_sh, b_sh = pltpu.roll(a, s, 0), pltpu.roll(b, s, 0)
            b = b + a * jnp.where(live, b_sh, 0.0)
            a = a * jnp.where(live, a_sh, 1.0)
        hs = b + a * entry
        last = hs[0:1, :] if reverse else hs[SUBLANES - 1:SUBLANES, :]
        return hs, jnp.broadcast_to(last, hs.shape)

    n_tile = t_len // SUBLANES

    def step(t, carry):
        fwd = pl.ds(pl.multiple_of(t * SUBLANES, SUBLANES), SUBLANES)
        bwd = pl.ds(pl.multiple_of((n_tile - 1 - t) * SUBLANES, SUBLANES), SUBLANES)
        out = []
        for j in range(n_blk):
            hs, ef = tile_scan(af_ref[j, fwd, :], bf_ref[j, fwd, :], carry[2 * j], False)
            bf_ref[j, fwd, :] = hs
            hs, eb = tile_scan(ab_ref[j, bwd, :], bb_ref[j, bwd, :], carry[2 * j + 1], True)
            bb_ref[j, bwd, :] = hs
            out += [ef, eb]
        return tuple(out)

    init = []
    for j in range(n_blk):
        sl = slice(j * RNN_BS, (j + 1) * RNN_BS)
        init += [jnp.broadcast_to(h0_ref[0:1, sl], (SUBLANES, RNN_BS)), jnp.broadcast_to(h0_ref[1:2, sl], (SUBLANES, RNN_BS))]
    carry = lax.fori_loop(0, n_tile, step, tuple(init), unroll=2)

    yr = _dot(h, wy_ref[...])
    for j in range(n_blk):
        sl = slice(j * RNN_BS, (j + 1) * RNN_BS)
        st_ref[0:1, sl] = carry[2 * j][0:1, :]
        st_ref[1:2, sl] = carry[2 * j + 1][0:1, :]
        oc_ref[:, sl] = (jax.nn.gelu(yr[:, sl]) * (bf_ref[j] + bb_ref[j])).astype(BF16)


def _rglru(l, h_all, w_xy, conv_w, conv_b, wg, bg, lam, h0, h0_map, *, n_seq, t_len, tok0, name):
    cb = RNN_CB
    nb = D_RNN // cb
    seq_blk0 = tok0 // t_len
    chan3 = lambda b, n: (l, 0, n)
    return pl.pallas_call(
        functools.partial(_rglru_kernel, t_len=t_len),
        grid=(n_seq, nb),
        in_specs=[
            pl.BlockSpec((t_len, D_MODEL), lambda b, n: (seq_blk0 + b, 0)),
            pl.BlockSpec((None, D_MODEL, cb), lambda b, n: (l, 0, OFF_XR // cb + n)),
            pl.BlockSpec((None, D_MODEL, cb), lambda b, n: (l, 0, OFF_XR // cb + nb + n)),
            pl.BlockSpec((None, 4, cb), chan3),
            pl.BlockSpec((None, 1, cb), chan3),
            pl.BlockSpec((None, cb // RNN_BS, RNN_BS, 4 * RNN_BS), lambda b, n: (l, n, 0, 0)),
            pl.BlockSpec((None, 4, cb), chan3),
            pl.BlockSpec((None, 2, cb), chan3),
            pl.BlockSpec((None, None, 2, cb), h0_map),
        ],
        out_specs=[
            pl.BlockSpec((t_len, cb), lambda b, n: (b, n)),
            pl.BlockSpec((None, 2, cb), lambda b, n: (b, 0, n)),
        ],
        out_shape=[
            jax.ShapeDtypeStruct((n_seq * t_len, D_RNN), BF16),
            jax.ShapeDtypeStruct((n_seq, 2, D_RNN), F32),
        ],
        scratch_shapes=[pltpu.VMEM((cb // RNN_BS, t_len, RNN_BS), F32)] * 4,
        compiler_params=_cparams(2),
        name=name,
    )(h_all, w_xy, w_xy, conv_w, conv_b, wg, bg, lam, h0)


def _merge_kernel(oac_ref, oal_ref, obc_ref, obl_ref, occ_ref, ocl_ref, h_ref, x_ref, mod_ref, gpost_ref, gpre2_ref,
                  woa_ref, wob_ref, woc_ref, wgl_ref, wout_ref, x1_ref, h2_ref):
    h = h_ref[...]
    is_ctx = pl.program_id(0) < N_CTX // TM_MERGE

    def gate(k):
        return jax.nn.sigmoid(_dot(h, wgl_ref[:, k * D_MODEL:(k + 1) * D_MODEL]))

    def branch(c_ref, l_ref, w_ref):
        return _dot(jnp.where(is_ctx, c_ref[...], l_ref[...]), w_ref[...])

    merged = gate(0) * branch(oac_ref, oal_ref, woa_ref)
    merged = merged + gate(1) * branch(obc_ref, obl_ref, wob_ref)
    merged = merged + gate(2) * branch(occ_ref, ocl_ref, woc_ref)
    out = _dot(merged.astype(BF16), wout_ref[...])
    gt1 = mod_ref[:, 2 * D_MODEL:3 * D_MODEL]
    sh2 = mod_ref[:, 3 * D_MODEL:4 * D_MODEL]
    sc2 = mod_ref[:, 4 * D_MODEL:5 * D_MODEL]
    x1 = x_ref[...] + gt1 * _rms(out, gpost_ref[...])
    x1_ref[...] = x1
    h2_ref[...] = (_rms(x1, gpre2_ref[...]) * (1.0 + sc2) + sh2).astype(BF16)


def _merge(l, oa, ob, oc, h, x, mods, gpost, gpre2, w_oa, w_ob, w_oc, w_gl, w_out):
    tm = TM_MERGE
    nct = N_CTX // tm
    row = lambda i: (i, 0)
    ctx = lambda i: (jnp.minimum(i, nct - 1), 0)
    lat = lambda i: (jnp.maximum(i - nct, 0), 0)
    layer = lambda i: (l, 0, 0)

    def pair(width):
        return [pl.BlockSpec((tm, width), ctx), pl.BlockSpec((tm, width), lat)]

    return pl.pallas_call(
        _merge_kernel,
        grid=(N_TOK // tm,),
        in_specs=pair(QA_W) + pair(OB_W) + pair(D_RNN) + [
            pl.BlockSpec((tm, D_MODEL), row),
            pl.BlockSpec((tm, D_MODEL), row),
            pl.BlockSpec((None, 1, 6 * D_MODEL), lambda i: (_mod_row(l, i, tm), 0, 0)),
            pl.BlockSpec((None, 1, D_MODEL), layer),
            pl.BlockSpec((None, 1, D_MODEL), layer),
            pl.BlockSpec((None, QA_W, D_MODEL), layer),
            pl.BlockSpec((None, OB_W, D_MODEL), layer),
            pl.BlockSpec((None, D_RNN, D_MODEL), layer),
            pl.BlockSpec((None, D_MODEL, 3 * D_MODEL), lambda i: (l, 0, OFF_GL // (3 * D_MODEL))),
            pl.BlockSpec((None, D_MODEL, D_MODEL), layer),
        ],
        out_specs=[pl.BlockSpec((tm, D_MODEL), row), pl.BlockSpec((tm, D_MODEL), row)],
        out_shape=[
            jax.ShapeDtypeStruct((N_TOK, D_MODEL), F32),
            jax.ShapeDtypeStruct((N_TOK, D_MODEL), BF16),
        ],
        compiler_params=_cparams(1),
        name="merge",
    )(*oa, *ob, *oc, h, x, mods, gpost, gpre2, w_oa, w_ob, w_oc, w_gl, w_out)


def _ffn_kernel(hp_ref, hm_ref, hn_ref, x1_ref, mod_ref, gpost_ref, wv_ref, wg_ref, cwv_ref, cwg_ref,
                cbv_ref, cbg_ref, wd_ref, o_ref, hext_ref, acc_ref, *, tile0):
    i = pl.program_id(0) + tile0
    c = pl.program_id(1)
    tm = TM_FFN

    @pl.when(c == 0)
    def _():
        hext_ref[0:HALO, :] = hp_ref[...]
        hext_ref[HALO:HALO + tm, :] = hm_ref[...]
        hext_ref[HALO + tm:, :] = hn_ref[...]
        acc_ref[...] = jnp.zeros_like(acc_ref)

    r = i * tm + lax.broadcasted_iota(jnp.int32, (tm, 1), 0)
    t = jnp.where(r < N_CTX, r & (SEQ - 1), r & (DEC_SEQ - 1))
    t_last = jnp.where(r < N_CTX, SEQ - 1, DEC_SEQ - 1)
    first = t == 0
    last = t == t_last
    hext = hext_ref[...]
    n_ext = tm + 2 * HALO

    def conv(w_ref, cw_ref, cb_ref):
        up = _dot(hext, w_ref[...])
        prev = jnp.where(first, 0.0, pltpu.roll(up, 1, 0)[HALO:HALO + tm])
        nxt = jnp.where(last, 0.0, pltpu.roll(up, n_ext - 1, 0)[HALO:HALO + tm])
        return cw_ref[0:1, :] * prev + cw_ref[1:2, :] * up[HALO:HALO + tm] + cw_ref[2:3, :] * nxt + cb_ref[...]

    val = conv(wv_ref, cwv_ref, cbv_ref)
    gat = conv(wg_ref, cwg_ref, cbg_ref)
    acc_ref[...] += _dot((jax.nn.gelu(gat) * val).astype(BF16), wd_ref[...])

    @pl.when(c == pl.num_programs(1) - 1)
    def _():
        gt2 = mod_ref[:, 5 * D_MODEL:6 * D_MODEL]
        o_ref[...] = x1_ref[...] + gt2 * _rms(acc_ref[...], gpost_ref[...])


def _ffn(l, h2, x1, mods, gpost, w_up, conv_w, conv_b, w_down, *, tok0, n_tok, name):
    tm, tf = TM_FFN, TF_FFN
    nc = D_FF // tf
    per = tm // HALO
    n_halo_blocks = N_TOK // HALO
    t0 = tok0 // tm
    row = lambda i, c: (t0 + i, 0)
    layer = lambda i, c: (l, 0, 0)
    return pl.pallas_call(
        functools.partial(_ffn_kernel, tile0=t0),
        grid=(n_tok // tm, nc),
        in_specs=[
            pl.BlockSpec((HALO, D_MODEL), lambda i, c: (jnp.maximum((t0 + i) * per - 1, 0), 0)),
            pl.BlockSpec((tm, D_MODEL), row),
            pl.BlockSpec((HALO, D_MODEL), lambda i, c: (jnp.minimum((t0 + i + 1) * per, n_halo_blocks - 1), 0)),
            pl.BlockSpec((tm, D_MODEL), row),
            pl.BlockSpec((None, 1, 6 * D_MODEL), lambda i, c: (_mod_row(l, t0 + i, tm), 0, 0)),
            pl.BlockSpec((None, 1, D_MODEL), layer),
            pl.BlockSpec((None, D_MODEL, tf), lambda i, c: (l, 0, c)),
            pl.BlockSpec((None, D_MODEL, tf), lambda i, c: (l, 0, nc + c)),
            pl.BlockSpec((None, 3, tf), lambda i, c: (l, 0, c)),
            pl.BlockSpec((None, 3, tf), lambda i, c: (l, 0, nc + c)),
            pl.BlockSpec((None, 1, tf), lambda i, c: (l, 0, c)),
            pl.BlockSpec((None, 1, tf), lambda i, c: (l, 0, nc + c)),
            pl.BlockSpec((None, tf, D_MODEL), lambda i, c: (l, c, 0)),
        ],
        out_specs=pl.BlockSpec((tm, D_MODEL), lambda i, c: (i, 0)),
        out_shape=jax.ShapeDtypeStruct((n_tok, D_MODEL), F32),
        scratch_shapes=[pltpu.VMEM((tm + 2 * HALO, D_MODEL), BF16), pltpu.VMEM((tm, D_MODEL), F32)],
        compiler_params=_cparams(2),
        name=name,
    )(h2, h2, h2, x1, mods, gpost, w_up, w_up, conv_w, conv_w, conv_b, conv_b, w_down)


def _rope_tables():
    t = jnp.arange(DEC_SEQ)
    row = (t // GRID_W).astype(F32)[:, None]
    col = (t % GRID_W).astype(F32)[:, None]

    def parts(dim):
        n = dim // 4
        inv = ROPE_THETA ** (-jnp.arange(n, dtype=F32) / n)
        ar, ac = row * inv, col * inv
        z = jnp.zeros_like(ar)
        cos = jnp.concatenate([jnp.cos(ar), jnp.cos(ar), jnp.cos(ac), jnp.cos(ac)], axis=-1)
        s_up = jnp.concatenate([-jnp.sin(ar), z, -jnp.sin(ac), z], axis=-1)
        s_dn = jnp.concatenate([z, jnp.sin(ar), z, jnp.sin(ac)], axis=-1)
        return cos, s_up, s_dn

    a = [jnp.tile(p, (1, LANES // HD_A)) for p in parts(HD_A)]
    pad = lambda p, fill: jnp.concatenate(
        [fill * jnp.ones((DEC_SEQ, KR_LANE), F32), p, fill * jnp.ones((DEC_SEQ, LANES - KR_LANE - ROPE_B), F32)], axis=-1)
    cb, sbu, sbd = parts(ROPE_B)
    b = [pad(cb, 1.0), pad(sbu, 0.0), pad(sbd, 0.0)]
    lat = jnp.concatenate(a + b, axis=-1)
    ident_blk = jnp.concatenate([jnp.ones((TM_IN, LANES), F32), jnp.zeros((TM_IN, 2 * LANES), F32)], axis=-1)
    ident = jnp.concatenate([ident_blk, ident_blk], axis=-1)
    return jnp.concatenate([ident, lat], axis=0)


def _pack_w_proj(w_in):
    qb = w_in[:, :, 768:1536].reshape(DEPTH, D_MODEL, H_B, NOPE_B + ROPE_B)
    qb = jnp.pad(qb, ((0, 0), (0, 0), (0, 0), (0, LANES - NOPE_B - ROPE_B))).reshape(DEPTH, D_MODEL, QB_PAD)
    kr = jnp.pad(w_in[:, :, 1792:1824], ((0, 0), (0, 0), (KR_LANE, LANES - KR_LANE - ROPE_B)))
    zeros = lambda n: jnp.zeros((DEPTH, D_MODEL, n), w_in.dtype)
    return jnp.concatenate([w_in[:, :, 0:768], qb, w_in[:, :, 1536:1792], kr, zeros(OFF_XR - ATT_W),
                            w_in[:, :, 1824:3872], zeros(OFF_GL - OFF_XR - 2 * D_RNN), w_in[:, :, 3872:6944]],
                           axis=-1).astype(BF16)


def _pack_w_kv(w_uk, w_uv):
    uk = w_uk.reshape(DEPTH, KV_RANK, H_B, NOPE_B)
    k_part = jnp.pad(uk, ((0, 0), (0, 0), (0, 0), (0, LANES - NOPE_B))).reshape(DEPTH, KV_RANK, QB_PAD)
    return jnp.concatenate([k_part, w_uv], axis=-1).astype(BF16)


def _cache_dup_heads(cache, with_ones):
    x = jnp.transpose(cache, (1, 3, 0, 2, 4)).astype(BF16)
    parts = [x, x]
    if with_ones:
        parts.append(jnp.broadcast_to((jnp.arange(LANES) == 0).astype(BF16), x.shape[:-1] + (LANES,)))
    return jnp.concatenate(parts, axis=-1).reshape(DEPTH, KV_A, DEC_BATCH * PAST_LEN, -1)


def kernel(x_prompt, x_sample, c, cache_gqa_k, cache_gqa_v, cache_mla_ckv, cache_mla_krope, state_rglru_fwd, state_rglru_bwd, c_ctx, w_ada, b_ada, g_pre_mix, g_post_mix, g_pre_ffn, g_post_ffn, w_in, g_qa, g_ka, g_ckv, w_uk, w_uv, conv_rnn_w, conv_rnn_b, w_rg, b_rg, w_ig, b_ig, lam, w_oa, w_ob, w_oc, w_out, w_up, conv_ffn_w, conv_ffn_b, w_down):
    x = jnp.concatenate([x_prompt.reshape(N_CTX, D_MODEL), x_sample.reshape(N_LAT, D_MODEL)], axis=0)
    cvec = jnp.concatenate([c_ctx[None, :], c, jnp.zeros((N_MOD_ROWS - 1 - DEC_BATCH, D_MODEL), F32)], axis=0)
    mods = _modulation(cvec, w_ada, b_ada).reshape(DEPTH * N_MOD_ROWS, 1, 6 * D_MODEL)

    vec = lambda g: g.reshape(DEPTH, 1, -1)
    tab = _rope_tables()
    seg = jnp.arange(QA_W) // HD_A
    bd = jnp.where(seg[:, None] == seg[None, :], 1.0 / HD_A, 0.0).astype(BF16)
    lane = jnp.arange(QB_PAD) % LANES
    place32 = (lane[None, :] == KR_LANE + jnp.arange(ROPE_B)[:, None]).astype(BF16)
    place = jnp.pad(place32, ((KR_LANE, LANES - KR_LANE - ROPE_B), (0, 0)))
    w_att = w_xy = w_gl = _pack_w_proj(w_in)
    w_kv = _pack_w_kv(w_uk, w_uv)
    gqa_t = jnp.tile(g_qa, (1, H_A)).reshape(DEPTH, 1, QA_W)
    gka_t = jnp.tile(g_ka, (1, KV_A)).reshape(DEPTH, 1, KA_W)
    wg = (0.5 * jnp.concatenate([w_rg[:, 0], w_ig[:, 0], w_rg[:, 1], w_ig[:, 1]], axis=-1)).astype(BF16)
    bg = 0.5 * jnp.stack([b_rg[:, 0], b_ig[:, 0], b_rg[:, 1], b_ig[:, 1]], axis=1)
    h0_lat = jnp.stack([state_rglru_fwd, state_rglru_bwd], axis=2)
    h0_zero = jnp.zeros((1, 1, 2, D_RNN), F32)
    w_oa_b, w_ob_b, w_oc_b, w_out_b = (w.astype(BF16) for w in (w_oa, w_ob, w_oc, w_out))
    w_up_b, w_down_b = w_up.astype(BF16), w_down.astype(BF16)
    conv_ffn_b3 = vec(conv_ffn_b)

    kc_a = _cache_dup_heads(cache_gqa_k, False)
    vc_a = _cache_dup_heads(cache_gqa_v, True)
    kc_b, vc_b = _kvup_cache(cache_mla_ckv, cache_mla_krope, w_kv, place32)

    lat_blk0 = N_CTX // DEC_SEQ
    gqa_k = gqa_v = lambda j: j // 2
    mla_ke = lambda j: 2 * j
    mla_ko = lambda j: 2 * j + 1
    mla_v = lambda j: j

    def new_seg(k, v, rows, blk0, fe, fo, fv):
        at = lambda f: lambda b, j: (f(j), blk0 + b, 0)
        return (k, v, (None, rows, LANES), (None, rows, 2 * LANES), at(fe), at(fo), at(fv))

    new_k, new_v, new_ckv, new_kr, new_fwd, new_bwd = [], [], [], [], [], []
    y = z = None
    for l in range(DEPTH):
        qa, qb, ka, va, kdup, vdup, ckv, kr, kb, vb, h = _inproj(
            l, x, mods, vec(g_pre_mix), tab, w_att, bd, gqa_t, gka_t, vec(g_ckv), w_kv, place)

        def cache_seg(k, v, fe, fo, fv):
            at = lambda f: lambda b, j: (l, f(j), b, 0)
            return (k, v, (None, None, PAST_LEN, LANES), (None, None, PAST_LEN, 2 * LANES), at(fe), at(fo), at(fv))

        gqa_idx = (gqa_k, gqa_k, gqa_v)
        gqa = dict(qe_lane=lambda j: j, qo_lane=lambda j: j, scale=1.0, mask_q=True)
        oa_ctx = _attention(qa, [new_seg(kdup, vdup, SEQ, 0, *gqa_idx)],
                            n_batch=BATCH, t_len=SEQ, tok0=0, name="gqa_ctx", **gqa)
        oa_lat = _attention(qa, [cache_seg(kc_a, vc_a, *gqa_idx), new_seg(kdup, vdup, DEC_SEQ, lat_blk0, *gqa_idx)],
                            n_batch=DEC_BATCH, t_len=DEC_SEQ, tok0=N_CTX, name="gqa_lat", **gqa)
        mla_idx = (mla_ke, mla_ko, mla_v)
        mla = dict(qe_lane=mla_ke, qo_lane=mla_ko, scale=(NOPE_B + ROPE_B) ** -0.5, mask_q=False)
        ob_ctx = _attention(qb, [new_seg(kb, vb, SEQ, 0, *mla_idx)],
                            n_batch=BATCH, t_len=SEQ, tok0=0, name="mla_ctx", **mla)
        ob_lat = _attention(qb, [cache_seg(kc_b, vc_b, *mla_idx), new_seg(kb, vb, DEC_SEQ, lat_blk0, *mla_idx)],
                            n_batch=DEC_BATCH, t_len=DEC_SEQ, tok0=N_CTX, name="mla_lat", **mla)

        rnn_args = (w_xy, conv_rnn_w, vec(conv_rnn_b), wg, bg, lam)
        oc_ctx, st_ctx = _rglru(l, h, *rnn_args, h0_zero, lambda b, n: (0, 0, 0, n),
                                n_seq=BATCH, t_len=SEQ, tok0=0, name="rglru_ctx")
        oc_lat, _ = _rglru(l, h, *rnn_args, h0_lat, lambda b, n: (b, l, 0, n),
                           n_seq=DEC_BATCH, t_len=DEC_SEQ, tok0=N_CTX, name="rglru_lat")

        x1, h2 = _merge(l, (oa_ctx, oa_lat), (ob_ctx, ob_lat), (oc_ctx, oc_lat), h, x, mods,
                        vec(g_post_mix), vec(g_pre_ffn), w_oa_b, w_ob_b, w_oc_b, w_gl, w_out_b)
        ffn_args = (l, h2, x1, mods, vec(g_post_ffn), w_up_b, conv_ffn_w, conv_ffn_b3, w_down_b)
        if l < DEPTH - 1:
            x = _ffn(*ffn_args, tok0=0, n_tok=N_TOK, name="ffn")
        else:
            y = _ffn(*ffn_args, tok0=0, n_tok=N_CTX, name="ffn_ctx")
            z = _ffn(*ffn_args, tok0=N_CTX, n_tok=N_LAT, name="ffn_lat")

        new_k.append(ka[:N_CTX].reshape(BATCH, SEQ, KV_A, HD_A))
        new_v.append(va[:N_CTX].reshape(BATCH, SEQ, KV_A, HD_A))
        new_ckv.append(ckv[:N_CTX].reshape(BATCH, SEQ, KV_RANK))
        new_kr.append(kr[:N_CTX].reshape(BATCH, SEQ, ROPE_B))
        new_fwd.append(st_ctx[:, 0])
        new_bwd.append(st_ctx[:, 1])

    stack = lambda xs: jnp.stack(xs, axis=1)
    return (y.reshape(BATCH, SEQ, D_MODEL), z.reshape(DEC_BATCH, DEC_SEQ, D_MODEL),
            stack(new_k), stack(new_v), stack(new_ckv), stack(new_kr), stack(new_fwd), stack(new_bwd))
```

```python
import functools
import math

import jax
import jax.numpy as jnp
import numpy as np
from jax import lax
from jax.experimental import pallas as pl
from jax.experimental.pallas import tpu as pltpu

F32 = jnp.float32
BF16 = jnp.bfloat16

D_MODEL = 1024
BATCH = 16
SEQ = 256
DEPTH = 2
DEC_BATCH = 4
DEC_SEQ = 2048
PAST_LEN = 512
GRID_W = 64
H_A = 8
KV_A = 2
HD_A = 64
H_B = 8
NOPE_B = 64
ROPE_B = 32
VD_B = 64
KV_RANK = 256
D_RNN = 1024
RNN_BLOCKS = 8
RNN_BS = D_RNN // RNN_BLOCKS
RG_C = 8.0
D_FF = 2816
ROPE_THETA = 10000.0
EPS = 1e-6
QA_W = H_A * HD_A
KA_W = KV_A * HD_A
QB_W = H_B * (NOPE_B + ROPE_B)
OB_W = H_B * VD_B

LANES = 128
SUBLANES = 8
N_CTX = BATCH * SEQ
N_LAT = DEC_BATCH * DEC_SEQ
N_TOK = N_CTX + N_LAT
N_MOD_ROWS = 8
QB_PAD = H_B * LANES
KR_LANE = NOPE_B
N_PAIR = 4
OFF_QA, OFF_KA, OFF_VA, OFF_QB, OFF_CKV, OFF_KR = 0, 512, 640, 768, 1792, 2048
ATT_W = OFF_KR + LANES
ATT_BLK = 2304
OFF_XR = ATT_BLK
OFF_GL = 6144
PROJ_W = OFF_GL + 3 * D_MODEL
KV_W = QB_PAD + OB_W
TAB_W = 6 * LANES
LOG2E = math.log2(math.e)

TM_IN = 256
TQ = 512
RNN_CB = 256
TM_MERGE = 512
TM_FFN = 512
TF_FFN = D_FF // 2
HALO = 16
VMEM_LIMIT = 56 * 1024 * 1024


def _cparams(n_axes):
    return pltpu.CompilerParams(dimension_semantics=("arbitrary",) * n_axes, vmem_limit_bytes=VMEM_LIMIT)


def _dot(a, b):
    return jnp.dot(a, b, preferred_element_type=F32)


def _rms(x, g):
    return x * lax.rsqrt(jnp.mean(x * x, axis=-1, keepdims=True) + EPS) * g


def _mod_row(l, i, tm):
    n_ctx_tiles = N_CTX // tm
    return l * N_MOD_ROWS + jnp.where(i < n_ctx_tiles, 0, 1 + (i - n_ctx_tiles) // (DEC_SEQ // tm))


def _mod_kernel(c_ref, w_ref, b_ref, o_ref):
    c = c_ref[...]
    s = (c * jax.nn.sigmoid(c)).astype(BF16)
    o_ref[...] = _dot(s, w_ref[...].astype(BF16)) + b_ref[...]


def _modulation(cvec, w_ada, b_ada):
    tn = 1536
    return pl.pallas_call(
        _mod_kernel,
        grid=(DEPTH, 6 * D_MODEL // tn),
        in_specs=[
            pl.BlockSpec((N_MOD_ROWS, D_MODEL), lambda l, n: (0, 0)),
            pl.BlockSpec((None, D_MODEL, tn), lambda l, n: (l, 0, n)),
            pl.BlockSpec((None, 1, tn), lambda l, n: (l, 0, n)),
        ],
        out_specs=pl.BlockSpec((None, N_MOD_ROWS, tn), lambda l, n: (l, 0, n)),
        out_shape=jax.ShapeDtypeStruct((DEPTH, N_MOD_ROWS, 6 * D_MODEL), F32),
        compiler_params=_cparams(2),
        name="modulation",
    )(cvec, w_ada, b_ada.reshape(DEPTH, 1, 6 * D_MODEL))


def _seg_mean(x2, bd):
    hi = x2.astype(BF16)
    lo = (x2 - hi.astype(F32)).astype(BF16)
    return _dot(hi, bd) + _dot(lo, bd)


def _rope(x, cos, sin_up, sin_dn, shift):
    w = x.shape[-1]
    return x * cos + pltpu.roll(x, w - shift, 1) * sin_up + pltpu.roll(x, shift, 1) * sin_dn


def _dup_heads(x):
    lo = lax.broadcasted_iota(jnp.int32, x.shape, 1) < HD_A
    sw = pltpu.roll(x, HD_A, 1)
    return jnp.where(lo, x, sw).astype(BF16), jnp.where(lo, sw, x).astype(BF16)


def _ones_column(rows):
    return jnp.where(lax.broadcasted_iota(jnp.int32, (rows, LANES), 1) == 0, 1.0, 0.0).astype(BF16)


def _store_mla_kv(y, kr_all, ones, k_ref, v_ref):
    for hd in range(H_B):
        sl = slice(hd * LANES, (hd + 1) * LANES)
        k_ref[hd] = (y[:, sl] + kr_all[:, sl]).astype(BF16)
    for j in range(N_PAIR):
        v_ref[j, :, 0:LANES] = y[:, QB_PAD + j * LANES:QB_PAD + (j + 1) * LANES].astype(BF16)
        v_ref[j, :, LANES:2 * LANES] = ones


def _inproj_kernel(xc_ref, xl_ref, mod_ref, gpre_ref, tab_ref, w_ref, bd_ref, gqa_ref, gka_ref, gckv_ref, wkv_ref, place_ref,
                   qa_ref, qb_ref, ka_ref, va_ref, kdup_ref, vdup_ref, ckv_ref, kr_ref, kb_ref, vb_ref, h_ref):
    x = jnp.where(pl.program_id(0) < N_CTX // TM_IN, xc_ref[...], xl_ref[...])
    sh1 = mod_ref[:, 0:D_MODEL]
    sc1 = mod_ref[:, D_MODEL:2 * D_MODEL]
    h = (_rms(x, gpre_ref[...]) * (1.0 + sc1) + sh1).astype(BF16)
    h_ref[...] = h
    y = _dot(h, w_ref[:, 0:ATT_W])

    cos_a, sa_up, sa_dn = tab_ref[:, 0:128], tab_ref[:, 128:256], tab_ref[:, 256:384]
    cos_b, sb_up, sb_dn = tab_ref[:, 384:512], tab_ref[:, 512:640], tab_ref[:, 640:768]

    q = y[:, OFF_QA:OFF_QA + QA_W]
    q = q * lax.rsqrt(_seg_mean(q * q, bd_ref[...]) + EPS) * gqa_ref[...]
    rep = QA_W // LANES
    q = _rope(q, jnp.tile(cos_a, (1, rep)), jnp.tile(sa_up, (1, rep)), jnp.tile(sa_dn, (1, rep)), HD_A // 4)
    qa_ref[...] = (q * (HD_A ** -0.5)).astype(BF16)

    k = y[:, OFF_KA:OFF_KA + KA_W]
    k = k * lax.rsqrt(_seg_mean(k * k, bd_ref[0:KA_W, 0:KA_W]) + EPS) * gka_ref[...]
    k = _rope(k, cos_a, sa_up, sa_dn, HD_A // 4)
    ka_ref[...] = k
    v = y[:, OFF_VA:OFF_VA + KA_W]
    va_ref[...] = v
    ones = _ones_column(k.shape[0])
    for n, (kd, vd) in enumerate(zip(_dup_heads(k), _dup_heads(v))):
        kdup_ref[n] = kd
        vdup_ref[n, :, 0:LANES] = vd
        vdup_ref[n, :, LANES:2 * LANES] = ones

    qb = y[:, OFF_QB:OFF_QB + QB_PAD]
    qb = _rope(qb, jnp.tile(cos_b, (1, H_B)), jnp.tile(sb_up, (1, H_B)), jnp.tile(sb_dn, (1, H_B)), ROPE_B // 4)
    qb_ref[...] = qb.astype(BF16)

    ckv = _rms(y[:, OFF_CKV:OFF_CKV + KV_RANK], gckv_ref[...])
    ckv_ref[...] = ckv
    kr = _rope(y[:, OFF_KR:OFF_KR + LANES], cos_b, sb_up, sb_dn, ROPE_B // 4)
    kr_ref[...] = kr[:, KR_LANE:KR_LANE + ROPE_B]

    y2 = _dot(ckv.astype(BF16), wkv_ref[...])
    kr_all = _dot(kr.astype(BF16), place_ref[...])
    _store_mla_kv(y2, kr_all, ones, kb_ref, vb_ref)


def _inproj(l, x_ctx, x_lat, mods, gpre, tab, w_att, bd, gqa_t, gka_t, gckv, w_kv, place):
    tm = TM_IN
    n_ctx_tiles = N_CTX // tm
    lat_tiles = DEC_SEQ // tm

    def tab_idx(i):
        return (jnp.where(i < n_ctx_tiles, 0, 1 + (i - n_ctx_tiles) % lat_tiles), 0)

    row = lambda i: (i, 0)
    row3 = lambda i: (0, i, 0)
    const = lambda i: (0, 0)
    layer = lambda i: (l, 0, 0)
    return pl.pallas_call(
        _inproj_kernel,
        grid=(N_TOK // tm,),
        in_specs=[
            pl.BlockSpec((tm, D_MODEL), lambda i: (jnp.minimum(i, n_ctx_tiles - 1), 0)),
            pl.BlockSpec((tm, D_MODEL), lambda i: (jnp.maximum(i - n_ctx_tiles, 0), 0)),
            pl.BlockSpec((None, 1, 6 * D_MODEL), lambda i: (_mod_row(l, i, tm), 0, 0)),
            pl.BlockSpec((None, 1, D_MODEL), layer),
            pl.BlockSpec((tm, TAB_W), tab_idx),
            pl.BlockSpec((None, D_MODEL, ATT_BLK), layer),
            pl.BlockSpec((QA_W, QA_W), const),
            pl.BlockSpec((None, 1, QA_W), layer),
            pl.BlockSpec((None, 1, KA_W), layer),
            pl.BlockSpec((None, 1, KV_RANK), layer),
            pl.BlockSpec((None, KV_RANK, KV_W), layer),
            pl.BlockSpec((LANES, QB_PAD), const),
        ],
        out_specs=[
            pl.BlockSpec((tm, QA_W), row),
            pl.BlockSpec((tm, QB_PAD), row),
            pl.BlockSpec((tm, KA_W), row),
            pl.BlockSpec((tm, KA_W), row),
            pl.BlockSpec((KV_A, tm, LANES), row3),
            pl.BlockSpec((KV_A, tm, 2 * LANES), row3),
            pl.BlockSpec((tm, KV_RANK), row),
            pl.BlockSpec((tm, ROPE_B), row),
            pl.BlockSpec((H_B, tm, LANES), row3),
            pl.BlockSpec((N_PAIR, tm, 2 * LANES), row3),
            pl.BlockSpec((tm, D_MODEL), row),
        ],
        out_shape=[
            jax.ShapeDtypeStruct((N_TOK, QA_W), BF16),
            jax.ShapeDtypeStruct((N_TOK, QB_PAD), BF16),
            jax.ShapeDtypeStruct((N_TOK, KA_W), F32),
            jax.ShapeDtypeStruct((N_TOK, KA_W), F32),
            jax.ShapeDtypeStruct((KV_A, N_TOK, LANES), BF16),
            jax.ShapeDtypeStruct((KV_A, N_TOK, 2 * LANES), BF16),
            jax.ShapeDtypeStruct((N_TOK, KV_RANK), F32),
            jax.ShapeDtypeStruct((N_TOK, ROPE_B), F32),
            jax.ShapeDtypeStruct((H_B, N_TOK, LANES), BF16),
            jax.ShapeDtypeStruct((N_PAIR, N_TOK, 2 * LANES), BF16),
            jax.ShapeDtypeStruct((N_TOK, D_MODEL), BF16),
        ],
        compiler_params=_cparams(1),
        name="inproj",
    )(x_ctx, x_lat, mods, gpre, tab, w_att, bd, gqa_t, gka_t, gckv, w_kv, place)


def _kvup_cache_kernel(c_ref, r_ref, w_ref, place_ref, k_ref, v_ref):
    y = _dot(c_ref[...].astype(BF16), w_ref[...])
    kr_all = _dot(r_ref[...].astype(BF16), place_ref[...])
    _store_mla_kv(y, kr_all, _ones_column(y.shape[0]), k_ref, v_ref)


def _kvup_cache(cache_ckv, cache_kr, w_kv, place32):
    rows = DEC_BATCH * PAST_LEN
    idx = lambda l, b: (l, 0, b, 0)
    return pl.pallas_call(
        _kvup_cache_kernel,
        grid=(DEPTH, DEC_BATCH),
        in_specs=[
            pl.BlockSpec((None, None, PAST_LEN, KV_RANK), lambda l, b: (b, l, 0, 0)),
            pl.BlockSpec((None, None, PAST_LEN, ROPE_B), lambda l, b: (b, l, 0, 0)),
            pl.BlockSpec((None, KV_RANK, KV_W), lambda l, b: (l, 0, 0)),
            pl.BlockSpec((ROPE_B, QB_PAD), lambda l, b: (0, 0)),
        ],
        out_specs=[pl.BlockSpec((None, H_B, PAST_LEN, LANES), idx), pl.BlockSpec((None, N_PAIR, PAST_LEN, 2 * LANES), idx)],
        out_shape=[jax.ShapeDtypeStruct((DEPTH, H_B, rows, LANES), BF16),
                   jax.ShapeDtypeStruct((DEPTH, N_PAIR, rows, 2 * LANES), BF16)],
        compiler_params=_cparams(2),
        name="kvup_cache",
    )(cache_ckv, cache_kr, w_kv, place32)


def _attn_kernel(*refs, n_seg, n_pair, c_exp, mask_q):
    per_pair = 2 + 3 * n_seg
    o_ref = refs[-1]
    lo = lax.broadcasted_iota(jnp.int32, (o_ref.shape[0], LANES), 1) < HD_A

    def scores(q, ks):
        return [lax.dot_general(q, k[...], (((1,), (1,)), ((), ())), preferred_element_type=F32) for k in ks]

    def attend(ss, vs):
        m = functools.reduce(jnp.maximum, [jnp.max(s, axis=-1, keepdims=True) for s in ss]) * c_exp
        full = functools.reduce(jnp.add, [_dot(jnp.exp2(s * c_exp - m).astype(BF16), v[...]) for s, v in zip(ss, vs)])
        return full[:, 0:LANES] / full[:, LANES:LANES + 1]

    for p in range(n_pair):
        r = refs[p * per_pair:(p + 1) * per_pair]
        k_refs, v_refs = r[2:2 + 2 * n_seg], r[2 + 2 * n_seg:]
        qe, qo = r[0][...], r[1][...]
        if mask_q:
            qe = jnp.where(lo, qe, jnp.zeros_like(qe))
            qo = jnp.where(lo, jnp.zeros_like(qo), qo)
        ss_e = scores(qe, k_refs[0::2])
        ss_o = scores(qo, k_refs[1::2])
        o_ref[:, p * LANES:(p + 1) * LANES] = jnp.where(lo, attend(ss_e, v_refs), attend(ss_o, v_refs)).astype(o_ref.dtype)


def _attention(q, segs, *, n_batch, t_len, tok0, qe_lane, qo_lane, scale, mask_q, pairs_per_step, name):
    tq = min(TQ, t_len)
    nq = t_len // tq
    q_blk0 = tok0 // tq
    npp = pairs_per_step

    def q_map(lane_fn, p):
        return lambda b, g, i: (q_blk0 + b * nq + i, lane_fn(g * npp + p))

    def kv_map(f, p):
        return lambda b, g, i: f(b, g * npp + p)

    in_specs, args = [], []
    for p in range(npp):
        in_specs += [pl.BlockSpec((tq, LANES), q_map(qe_lane, p)), pl.BlockSpec((tq, LANES), q_map(qo_lane, p))]
        args += [q, q]
        for k, _, k_blk, _, ke_idx, ko_idx, _ in segs:
            in_specs += [pl.BlockSpec(k_blk, kv_map(ke_idx, p)), pl.BlockSpec(k_blk, kv_map(ko_idx, p))]
            args += [k, k]
        for _, v, _, v_blk, _, _, v_idx in segs:
            in_specs.append(pl.BlockSpec(v_blk, kv_map(v_idx, p)))
            args.append(v)
    return pl.pallas_call(
        functools.partial(_attn_kernel, n_seg=len(segs), n_pair=npp, c_exp=scale * LOG2E, mask_q=mask_q),
        grid=(n_batch, N_PAIR // npp, nq),
        in_specs=in_specs,
        out_specs=pl.BlockSpec((tq, npp * LANES), lambda b, g, i: (b * nq + i, g)),
        out_shape=jax.ShapeDtypeStruct((n_batch * t_len, N_PAIR * LANES), BF16),
        compiler_params=_cparams(3),
        name=name,
    )(*args)


def _rglru_kernel(h_ref, wx_ref, wy_ref, cw_ref, cb_ref, wg_ref, bg_ref, lam_ref, h0_ref,
                  oc_ref, st_ref, af_ref, bf_ref, ab_ref, bb_ref, *, t_len):
    h = h_ref[...]
    xr = _dot(h, wx_ref[...])
    sub = lax.broadcasted_iota(jnp.int32, (SUBLANES, 1), 0)

    def shifted(shift, keep_first, keep_last):
        x = pltpu.roll(xr, shift % t_len, 0)
        head = jnp.where(keep_first, x[:SUBLANES], 0.0)
        tail = jnp.where(keep_last, x[t_len - SUBLANES:], 0.0)
        return jnp.concatenate([head, x[SUBLANES:t_len - SUBLANES], tail], axis=0)

    x_m2 = shifted(2, sub >= 2, True)
    x_m1 = shifted(1, sub >= 1, True)
    x_p1 = shifted(-1, True, sub < SUBLANES - 1)
    u = cw_ref[0:1, :] * x_m2 + cw_ref[1:2, :] * x_m1 + cw_ref[2:3, :] * xr + cw_ref[3:4, :] * x_p1 + cb_ref[...]

    lam = lam_ref[...]
    half_c = (0.5 * RG_C) * (jnp.minimum(lam, 0.0) - jnp.log1p(jnp.exp(-jnp.abs(lam))))
    n_blk = RNN_CB // RNN_BS
    for j in range(n_blk):
        sl = slice(j * RNN_BS, (j + 1) * RNN_BS)
        uj = u[:, sl]
        g = _dot(uj.astype(BF16), wg_ref[j])
        for d, (a_ref, b_ref) in enumerate(((af_ref, bf_ref), (ab_ref, bb_ref))):
            tr = jnp.tanh(g[:, (2 * d) * RNN_BS:(2 * d + 1) * RNN_BS] + bg_ref[2 * d:2 * d + 1, sl])
            ti = jnp.tanh(g[:, (2 * d + 1) * RNN_BS:(2 * d + 2) * RNN_BS] + bg_ref[2 * d + 1:2 * d + 2, sl])
            c = half_c[d:d + 1, sl]
            a = jnp.exp(c * tr + c)
            a_ref[j] = a
            om = 1.0 - a * a
            root = jnp.where(om > 0.0, om * lax.rsqrt(om), 0.0)
            b_ref[j] = root * ((0.5 * ti + 0.5) * uj)

    row = lax.broadcasted_iota(jnp.int32, (SUBLANES, RNN_BS), 0)
    shifts = [s for s in (1, 2, 4) if s < SUBLANES]

    def tile_scan(a, b, entry, reverse):
        for s in shifts:
            if reverse:
                live = row < SUBLANES - s
                a_sh, b_sh = pltpu.roll(a, SUBLANES - s, 0), pltpu.roll(b, SUBLANES - s, 0)
            else:
                live = row >= s
                a_sh, b_sh = pltpu.roll(a, s, 0), pltpu.roll(b, s, 0)
            b = b + a * jnp.where(live, b_sh, 0.0)
            a = a * jnp.where(live, a_sh, 1.0)
        hs = b + a * entry
        last = hs[0:1, :] if reverse else hs[SUBLANES - 1:SUBLANES, :]
        return hs, jnp.broadcast_to(last, hs.shape)

    n_tile = t_len // SUBLANES

    def step(t, carry):
        fwd = pl.ds(pl.multiple_of(t * SUBLANES, SUBLANES), SUBLANES)
        bwd = pl.ds(pl.multiple_of((n_tile - 1 - t) * SUBLANES, SUBLANES), SUBLANES)
        out = []
        for j in range(n_blk):
            hs, ef = tile_scan(af_ref[j, fwd, :], bf_ref[j, fwd, :], carry[2 * j], False)
            bf_ref[j, fwd, :] = hs
            hs, eb = tile_scan(ab_ref[j, bwd, :], bb_ref[j, bwd, :], carry[2 * j + 1], True)
            bb_ref[j, bwd, :] = hs
            out += [ef, eb]
        return tuple(out)

    init = []
    for j in range(n_blk):
        sl = slice(j * RNN_BS, (j + 1) * RNN_BS)
        init += [jnp.broadcast_to(h0_ref[0:1, sl], (SUBLANES, RNN_BS)), jnp.broadcast_to(h0_ref[1:2, sl], (SUBLANES, RNN_BS))]
    carry = lax.fori_loop(0, n_tile, step, tuple(init), unroll=2)

    yr = _dot(h, wy_ref[...])
    for j in range(n_blk):
        sl = slice(j * RNN_BS, (j + 1) * RNN_BS)
        st_ref[0:1, sl] = carry[2 * j][0:1, :]
        st_ref[1:2, sl] = carry[2 * j + 1][0:1, :]
        oc_ref[:, sl] = (jax.nn.gelu(yr[:, sl]) * (bf_ref[j] + bb_ref[j])).astype(BF16)


def _rglru(l, h_all, w_xy, conv_w, conv_b, wg, bg, lam, h0, h0_map, *, n_seq, t_len, tok0, name):
    cb = RNN_CB
    nb = D_RNN // cb
    seq_blk0 = tok0 // t_len
    chan3 = lambda b, n: (l, 0, n)
    return pl.pallas_call(
        functools.partial(_rglru_kernel, t_len=t_len),
        grid=(n_seq, nb),
        in_specs=[
            pl.BlockSpec((t_len, D_MODEL), lambda b, n: (seq_blk0 + b, 0)),
            pl.BlockSpec((None, D_MODEL, cb), lambda b, n: (l, 0, OFF_XR // cb + n)),
            pl.BlockSpec((None, D_MODEL, cb), lambda b, n: (l, 0, OFF_XR // cb + nb + n)),
            pl.BlockSpec((None, 4, cb), chan3),
            pl.BlockSpec((None, 1, cb), chan3),
            pl.BlockSpec((None, cb // RNN_BS, RNN_BS, 4 * RNN_BS), lambda b, n: (l, n, 0, 0)),
            pl.BlockSpec((None, 4, cb), chan3),
            pl.BlockSpec((None, 2, cb), chan3),
            pl.BlockSpec((None, None, 2, cb), h0_map),
        ],
        out_specs=[
            pl.BlockSpec((t_len, cb), lambda b, n: (b, n)),
            pl.BlockSpec((None, 2, cb), lambda b, n: (b, 0, n)),
        ],
        out_shape=[
            jax.ShapeDtypeStruct((n_seq * t_len, D_RNN), BF16),
            jax.ShapeDtypeStruct((n_seq, 2, D_RNN), F32),
        ],
        scratch_shapes=[pltpu.VMEM((cb // RNN_BS, t_len, RNN_BS), F32)] * 4,
        compiler_params=_cparams(2),
        name=name,
    )(h_all, w_xy, w_xy, conv_w, conv_b, wg, bg, lam, h0)


def _merge_kernel(oac_ref, oal_ref, obc_ref, obl_ref, occ_ref, ocl_ref, xc_ref, xl_ref, h_ref, mod_ref, gpost_ref, gpre2_ref,
                  woa_ref, wob_ref, woc_ref, wgl_ref, wout_ref, x1_ref, h2_ref):
    h = h_ref[...]
    is_ctx = pl.program_id(0) < N_CTX // TM_MERGE

    def gate(k):
        return jax.nn.sigmoid(_dot(h, wgl_ref[:, k * D_MODEL:(k + 1) * D_MODEL]))

    def branch(c_ref, l_ref, w_ref):
        return _dot(jnp.where(is_ctx, c_ref[...], l_ref[...]), w_ref[...])

    merged = gate(0) * branch(oac_ref, oal_ref, woa_ref)
    merged = merged + gate(1) * branch(obc_ref, obl_ref, wob_ref)
    merged = merged + gate(2) * branch(occ_ref, ocl_ref, woc_ref)
    out = _dot(merged.astype(BF16), wout_ref[...])
    gt1 = mod_ref[:, 2 * D_MODEL:3 * D_MODEL]
    sh2 = mod_ref[:, 3 * D_MODEL:4 * D_MODEL]
    sc2 = mod_ref[:, 4 * D_MODEL:5 * D_MODEL]
    x1 = jnp.where(is_ctx, xc_ref[...], xl_ref[...]) + gt1 * _rms(out, gpost_ref[...])
    x1_ref[...] = x1
    h2_ref[...] = (_rms(x1, gpre2_ref[...]) * (1.0 + sc2) + sh2).astype(BF16)


def _merge(l, oa, ob, oc, x, h, mods, gpost, gpre2, w_oa, w_ob, w_oc, w_gl, w_out):
    tm = TM_MERGE
    nct = N_CTX // tm
    row = lambda i: (i, 0)
    ctx = lambda i: (jnp.minimum(i, nct - 1), 0)
    lat = lambda i: (jnp.maximum(i - nct, 0), 0)
    layer = lambda i: (l, 0, 0)

    def pair(width):
        return [pl.BlockSpec((tm, width), ctx), pl.BlockSpec((tm, width), lat)]

    return pl.pallas_call(
        _merge_kernel,
        grid=(N_TOK // tm,),
        in_specs=pair(QA_W) + pair(OB_W) + pair(D_RNN) + pair(D_MODEL) + [
            pl.BlockSpec((tm, D_MODEL), row),
            pl.BlockSpec((None, 1, 6 * D_MODEL), lambda i: (_mod_row(l, i, tm), 0, 0)),
            pl.BlockSpec((None, 1, D_MODEL), layer),
            pl.BlockSpec((None, 1, D_MODEL), layer),
            pl.BlockSpec((None, QA_W, D_MODEL), layer),
            pl.BlockSpec((None, OB_W, D_MODEL), layer),
            pl.BlockSpec((None, D_RNN, D_MODEL), layer),
            pl.BlockSpec((None, D_MODEL, 3 * D_MODEL), lambda i: (l, 0, OFF_GL // (3 * D_MODEL))),
            pl.BlockSpec((None, D_MODEL, D_MODEL), layer),
        ],
        out_specs=[pl.BlockSpec((tm, D_MODEL), row), pl.BlockSpec((tm, D_MODEL), row)],
        out_shape=[
            jax.ShapeDtypeStruct((N_TOK, D_MODEL), F32),
            jax.ShapeDtypeStruct((N_TOK, D_MODEL), BF16),
        ],
        compiler_params=_cparams(1),
        name="merge",
    )(*oa, *ob, *oc, *x, h, mods, gpost, gpre2, w_oa, w_ob, w_oc, w_gl, w_out)


def _ffn_kernel(hp_ref, hm_ref, hn_ref, x1_ref, mod_ref, gpost_ref, wu_ref, cw_ref, cb_ref, wd_ref, o_ref, hext_ref,
                *, t_len):
    tm = TM_FFN
    if t_len >= tm:
        tiles_per_seq = t_len // tm
        pos = pl.program_id(0) % tiles_per_seq
        hext_ref[0:HALO, :] = jnp.where(pos == 0, jnp.zeros_like(hp_ref[...]), hp_ref[...])
        hext_ref[HALO + tm:, :] = jnp.where(pos == tiles_per_seq - 1, jnp.zeros_like(hn_ref[...]), hn_ref[...])
        inner = []
    else:
        hext_ref[0:HALO, :] = jnp.zeros((HALO, D_MODEL), BF16)
        hext_ref[HALO + tm:, :] = jnp.zeros((HALO, D_MODEL), BF16)
        inner = list(range(t_len, tm, t_len))
    hext_ref[HALO:HALO + tm, :] = hm_ref[...]
    hext = hext_ref[...]
    n_ext = tm + 2 * HALO
    sub = lax.broadcasted_iota(jnp.int32, (SUBLANES, 1), 0)

    def patch(x, row):
        s0 = row // SUBLANES * SUBLANES
        slab = jnp.where(sub == row - s0, 0.0, x[s0:s0 + SUBLANES])
        return jnp.concatenate([x[:s0], slab, x[s0 + SUBLANES:]], axis=0)

    def conv(cols):
        up = _dot(hext, wu_ref[:, cols])
        prev = pltpu.roll(up, 1, 0)[HALO:HALO + tm]
        nxt = pltpu.roll(up, n_ext - 1, 0)[HALO:HALO + tm]
        for r in inner:
            prev = patch(prev, r)
            nxt = patch(nxt, r - 1)
        return cw_ref[0:1, cols] * prev + cw_ref[1:2, cols] * up[HALO:HALO + tm] + cw_ref[2:3, cols] * nxt + cb_ref[:, cols]

    acc = jnp.zeros((tm, D_MODEL), F32)
    for c in range(D_FF // TF_FFN):
        val = conv(slice(c * TF_FFN, (c + 1) * TF_FFN))
        gat = conv(slice(D_FF + c * TF_FFN, D_FF + (c + 1) * TF_FFN))
        acc = acc + _dot((jax.nn.gelu(gat) * val).astype(BF16), wd_ref[c * TF_FFN:(c + 1) * TF_FFN, :])
    gt2 = mod_ref[:, 5 * D_MODEL:6 * D_MODEL]
    o_ref[...] = x1_ref[...] + gt2 * _rms(acc, gpost_ref[...])


def _ffn(l, h2, x1, mods, gpost, w_up, conv_w, conv_b, w_down, *, tok0, n_tok, t_len, name):
    tm = TM_FFN
    per = tm // HALO
    n_halo_blocks = N_TOK // HALO
    t0 = tok0 // tm
    row = lambda i: (t0 + i, 0)
    resident = lambda shape: pl.BlockSpec(shape, lambda i: (l, 0, 0), pipeline_mode=pl.Buffered(1))
    return pl.pallas_call(
        functools.partial(_ffn_kernel, t_len=t_len),
        grid=(n_tok // tm,),
        in_specs=[
            pl.BlockSpec((HALO, D_MODEL), lambda i: (jnp.maximum((t0 + i) * per - 1, 0), 0)),
            pl.BlockSpec((tm, D_MODEL), row),
            pl.BlockSpec((HALO, D_MODEL), lambda i: (jnp.minimum((t0 + i + 1) * per, n_halo_blocks - 1), 0)),
            pl.BlockSpec((tm, D_MODEL), row),
            pl.BlockSpec((None, 1, 6 * D_MODEL), lambda i: (_mod_row(l, t0 + i, tm), 0, 0)),
            pl.BlockSpec((None, 1, D_MODEL), lambda i: (l, 0, 0)),
            resident((None, D_MODEL, 2 * D_FF)),
            resident((None, 3, 2 * D_FF)),
            resident((None, 1, 2 * D_FF)),
            resident((None, D_FF, D_MODEL)),
        ],
        out_specs=pl.BlockSpec((tm, D_MODEL), lambda i: (i, 0)),
        out_shape=jax.ShapeDtypeStruct((n_tok, D_MODEL), F32),
        scratch_shapes=[pltpu.VMEM((tm + 2 * HALO, D_MODEL), BF16)],
        compiler_params=_cparams(1),
        name=name,
    )(h2, h2, h2, x1, mods, gpost, w_up, conv_w, conv_b, w_down)


def _rope_tables():
    t = np.arange(DEC_SEQ)
    row = (t // GRID_W).astype(np.float64)[:, None]
    col = (t % GRID_W).astype(np.float64)[:, None]

    def parts(dim):
        n = dim // 4
        inv = ROPE_THETA ** (-np.arange(n, dtype=np.float64) / n)
        ar, ac = row * inv, col * inv
        z = np.zeros_like(ar)
        cos = np.concatenate([np.cos(ar), np.cos(ar), np.cos(ac), np.cos(ac)], axis=-1)
        s_up = np.concatenate([-np.sin(ar), z, -np.sin(ac), z], axis=-1)
        s_dn = np.concatenate([z, np.sin(ar), z, np.sin(ac)], axis=-1)
        return cos, s_up, s_dn

    a = [np.tile(p, (1, LANES // HD_A)) for p in parts(HD_A)]
    pad = lambda p, fill: np.concatenate(
        [np.full((DEC_SEQ, KR_LANE), fill), p, np.full((DEC_SEQ, LANES - KR_LANE - ROPE_B), fill)], axis=-1)
    cb, sbu, sbd = parts(ROPE_B)
    lat = np.concatenate(a + [pad(cb, 1.0), pad(sbu, 0.0), pad(sbd, 0.0)], axis=-1)
    ident_blk = np.concatenate([np.ones((TM_IN, LANES)), np.zeros((TM_IN, 2 * LANES))], axis=-1)
    ident = np.concatenate([ident_blk, ident_blk], axis=-1)
    return jnp.asarray(np.concatenate([ident, lat], axis=0), F32)


def _pack_w_proj(w_in):
    qb = w_in[:, :, 768:1536].reshape(DEPTH, D_MODEL, H_B, NOPE_B + ROPE_B)
    qb = jnp.pad(qb, ((0, 0), (0, 0), (0, 0), (0, LANES - NOPE_B - ROPE_B))).reshape(DEPTH, D_MODEL, QB_PAD)
    kr = jnp.pad(w_in[:, :, 1792:1824], ((0, 0), (0, 0), (KR_LANE, LANES - KR_LANE - ROPE_B)))
    zeros = lambda n: jnp.zeros((DEPTH, D_MODEL, n), w_in.dtype)
    return jnp.concatenate([w_in[:, :, 0:768], qb, w_in[:, :, 1536:1792], kr, zeros(OFF_XR - ATT_W),
                            w_in[:, :, 1824:3872], zeros(OFF_GL - OFF_XR - 2 * D_RNN), w_in[:, :, 3872:6944]],
                           axis=-1).astype(BF16)


def _pack_w_kv(w_uk, w_uv):
    uk = w_uk.reshape(DEPTH, KV_RANK, H_B, NOPE_B)
    k_part = jnp.pad(uk, ((0, 0), (0, 0), (0, 0), (0, LANES - NOPE_B))).reshape(DEPTH, KV_RANK, QB_PAD)
    return jnp.concatenate([k_part, w_uv], axis=-1).astype(BF16)


def _cache_dup_heads(cache, with_ones):
    x = jnp.transpose(cache, (1, 3, 0, 2, 4)).astype(BF16)
    parts = [x, x]
    if with_ones:
        parts.append(jnp.broadcast_to((jnp.arange(LANES) == 0).astype(BF16), x.shape[:-1] + (LANES,)))
    return jnp.concatenate(parts, axis=-1).reshape(DEPTH, KV_A, DEC_BATCH * PAST_LEN, -1)


def kernel(x_prompt, x_sample, c, cache_gqa_k, cache_gqa_v, cache_mla_ckv, cache_mla_krope, state_rglru_fwd, state_rglru_bwd, c_ctx, w_ada, b_ada, g_pre_mix, g_post_mix, g_pre_ffn, g_post_ffn, w_in, g_qa, g_ka, g_ckv, w_uk, w_uv, conv_rnn_w, conv_rnn_b, w_rg, b_rg, w_ig, b_ig, lam, w_oa, w_ob, w_oc, w_out, w_up, conv_ffn_w, conv_ffn_b, w_down):
    x = (x_prompt.reshape(N_CTX, D_MODEL), x_sample.reshape(N_LAT, D_MODEL))
    cvec = jnp.concatenate([c_ctx[None, :], c, jnp.zeros((N_MOD_ROWS - 1 - DEC_BATCH, D_MODEL), F32)], axis=0)
    mods = _modulation(cvec, w_ada, b_ada).reshape(DEPTH * N_MOD_ROWS, 1, 6 * D_MODEL)

    vec = lambda g: g.reshape(DEPTH, 1, -1)
    tab = _rope_tables()
    seg = np.arange(QA_W) // HD_A
    bd = jnp.asarray(np.where(seg[:, None] == seg[None, :], 1.0 / HD_A, 0.0), BF16)
    lane = np.arange(QB_PAD) % LANES
    place32_np = (lane[None, :] == KR_LANE + np.arange(ROPE_B)[:, None]).astype(np.float32)
    place32 = jnp.asarray(place32_np, BF16)
    place = jnp.asarray(np.pad(place32_np, ((KR_LANE, LANES - KR_LANE - ROPE_B), (0, 0))), BF16)
    w_att = w_xy = w_gl = _pack_w_proj(w_in)
    w_kv = _pack_w_kv(w_uk, w_uv)
    gqa_t = jnp.tile(g_qa, (1, H_A)).reshape(DEPTH, 1, QA_W)
    gka_t = jnp.tile(g_ka, (1, KV_A)).reshape(DEPTH, 1, KA_W)
    wg = (0.5 * jnp.concatenate([w_rg[:, 0], w_ig[:, 0], w_rg[:, 1], w_ig[:, 1]], axis=-1)).astype(BF16)
    bg = 0.5 * jnp.stack([b_rg[:, 0], b_ig[:, 0], b_rg[:, 1], b_ig[:, 1]], axis=1)
    h0_lat = jnp.stack([state_rglru_fwd, state_rglru_bwd], axis=2)
    h0_zero = jnp.zeros((1, 1, 2, D_RNN), F32)
    w_oa_b, w_ob_b, w_oc_b, w_out_b = (w.astype(BF16) for w in (w_oa, w_ob, w_oc, w_out))
    w_up_b, w_down_b = w_up.astype(BF16), w_down.astype(BF16)
    conv_ffn_b3 = vec(conv_ffn_b)

    kc_a = _cache_dup_heads(cache_gqa_k, False)
    vc_a = _cache_dup_heads(cache_gqa_v, True)
    kc_b, vc_b = _kvup_cache(cache_mla_ckv, cache_mla_krope, w_kv, place32)

    lat_blk0 = N_CTX // DEC_SEQ
    gqa_k = gqa_v = lambda j: j // 2
    mla_ke = lambda j: 2 * j
    mla_ko = lambda j: 2 * j + 1
    mla_v = lambda j: j

    def new_seg(k, v, rows, blk0, fe, fo, fv):
        at = lambda f: lambda b, j: (f(j), blk0 + b, 0)
        return (k, v, (None, rows, LANES), (None, rows, 2 * LANES), at(fe), at(fo), at(fv))

    new_k, new_v, new_ckv, new_kr, new_fwd, new_bwd = [], [], [], [], [], []
    for l in range(DEPTH):
        qa, qb, ka, va, kdup, vdup, ckv, kr, kb, vb, h = _inproj(
            l, *x, mods, vec(g_pre_mix), tab, w_att, bd, gqa_t, gka_t, vec(g_ckv), w_kv, place)

        def cache_seg(k, v, fe, fo, fv):
            at = lambda f: lambda b, j: (l, f(j), b, 0)
            return (k, v, (None, None, PAST_LEN, LANES), (None, None, PAST_LEN, 2 * LANES), at(fe), at(fo), at(fv))

        gqa_idx = (gqa_k, gqa_k, gqa_v)
        gqa = dict(qe_lane=lambda j: j, qo_lane=lambda j: j, scale=1.0, mask_q=True)
        oa_ctx = _attention(qa, [new_seg(kdup, vdup, SEQ, 0, *gqa_idx)],
                            n_batch=BATCH, t_len=SEQ, tok0=0, pairs_per_step=N_PAIR, name="gqa_ctx", **gqa)
        oa_lat = _attention(qa, [cache_seg(kc_a, vc_a, *gqa_idx), new_seg(kdup, vdup, DEC_SEQ, lat_blk0, *gqa_idx)],
                            n_batch=DEC_BATCH, t_len=DEC_SEQ, tok0=N_CTX, pairs_per_step=1, name="gqa_lat", **gqa)
        mla_idx = (mla_ke, mla_ko, mla_v)
        mla = dict(qe_lane=mla_ke, qo_lane=mla_ko, scale=(NOPE_B + ROPE_B) ** -0.5, mask_q=False)
        ob_ctx = _attention(qb, [new_seg(kb, vb, SEQ, 0, *mla_idx)],
                            n_batch=BATCH, t_len=SEQ, tok0=0, pairs_per_step=N_PAIR, name="mla_ctx", **mla)
        ob_lat = _attention(qb, [cache_seg(kc_b, vc_b, *mla_idx), new_seg(kb, vb, DEC_SEQ, lat_blk0, *mla_idx)],
                            n_batch=DEC_BATCH, t_len=DEC_SEQ, tok0=N_CTX, pairs_per_step=1, name="mla_lat", **mla)

        rnn_args = (w_xy, conv_rnn_w, vec(conv_rnn_b), wg, bg, lam)
        oc_ctx, st_ctx = _rglru(l, h, *rnn_args, h0_zero, lambda b, n: (0, 0, 0, n),
                                n_seq=BATCH, t_len=SEQ, tok0=0, name="rglru_ctx")
        oc_lat, _ = _rglru(l, h, *rnn_args, h0_lat, lambda b, n: (b, l, 0, n),
                           n_seq=DEC_BATCH, t_len=DEC_SEQ, tok0=N_CTX, name="rglru_lat")

        x1, h2 = _merge(l, (oa_ctx, oa_lat), (ob_ctx, ob_lat), (oc_ctx, oc_lat), x, h, mods,
                        vec(g_post_mix), vec(g_pre_ffn), w_oa_b, w_ob_b, w_oc_b, w_gl, w_out_b)
        ffn_args = (l, h2, x1, mods, vec(g_post_ffn), w_up_b, conv_ffn_w, conv_ffn_b3, w_down_b)
        x = (_ffn(*ffn_args, tok0=0, n_tok=N_CTX, t_len=SEQ, name="ffn_ctx"),
             _ffn(*ffn_args, tok0=N_CTX, n_tok=N_LAT, t_len=DEC_SEQ, name="ffn_lat"))

        new_k.append(ka[:N_CTX].reshape(BATCH, SEQ, KV_A, HD_A))
        new_v.append(va[:N_CTX].reshape(BATCH, SEQ, KV_A, HD_A))
        new_ckv.append(ckv[:N_CTX].reshape(BATCH, SEQ, KV_RANK))
        new_kr.append(kr[:N_CTX].reshape(BATCH, SEQ, ROPE_B))
        new_fwd.append(st_ctx[:, 0])
        new_bwd.append(st_ctx[:, 1])

    stack = lambda xs: jnp.stack(xs, axis=1)
    return (x[0].reshape(BATCH, SEQ, D_MODEL), x[1].reshape(DEC_BATCH, DEC_SEQ, D_MODEL),
            stack(new_k), stack(new_v), stack(new_ckv), stack(new_kr), stack(new_fwd), stack(new_bwd))
```

```python
import functools
import math

import jax
import jax.numpy as jnp
import numpy as np
from jax import lax
from jax.experimental import pallas as pl
from jax.experimental.pallas import tpu as pltpu

F32 = jnp.float32
BF16 = jnp.bfloat16

D_MODEL = 1024
BATCH = 16
SEQ = 256
DEPTH = 2
DEC_BATCH = 4
DEC_SEQ = 2048
PAST_LEN = 512
GRID_W = 64
H_A = 8
KV_A = 2
HD_A = 64
H_B = 8
NOPE_B = 64
ROPE_B = 32
VD_B = 64
KV_RANK = 256
D_RNN = 1024
RNN_BLOCKS = 8
RNN_BS = D_RNN // RNN_BLOCKS
RG_C = 8.0
D_FF = 2816
ROPE_THETA = 10000.0
EPS = 1e-6
QA_W = H_A * HD_A
KA_W = KV_A * HD_A
QB_W = H_B * (NOPE_B + ROPE_B)
OB_W = H_B * VD_B

LANES = 128
SUBLANES = 8
N_CTX = BATCH * SEQ
N_LAT = DEC_BATCH * DEC_SEQ
N_TOK = N_CTX + N_LAT
N_MOD_ROWS = 8
QB_PAD = H_B * LANES
KR_LANE = NOPE_B
N_PAIR = 4
OFF_QA, OFF_KA, OFF_VA, OFF_QB, OFF_CKV, OFF_KR = 0, 512, 640, 768, 1792, 2048
ATT_W = OFF_KR + LANES
ATT_BLK = 2304
OFF_XR = ATT_BLK
OFF_GL = 6144
PROJ_W = OFF_GL + 3 * D_MODEL
KV_W = QB_PAD + OB_W
TAB_W = 6 * LANES
LOG2E = math.log2(math.e)

TM_IN = 256
TQ = 512
RNN_CB = 256
RNN_CTX_SUB = 4
TM_MERGE = 512
TM_FFN = 512
FFN_CHUNKS = ((0, 1536), (1536, D_FF))
HALO = 16
VMEM_LIMIT = 56 * 1024 * 1024


def _cparams(n_axes):
    return pltpu.CompilerParams(dimension_semantics=("arbitrary",) * n_axes, vmem_limit_bytes=VMEM_LIMIT)


def _dot(a, b):
    return jnp.dot(a, b, preferred_element_type=F32)


def _rms(x, g):
    return x * lax.rsqrt(jnp.mean(x * x, axis=-1, keepdims=True) + EPS) * g


def _mod_row(l, i, tm):
    n_ctx_tiles = N_CTX // tm
    return l * N_MOD_ROWS + jnp.where(i < n_ctx_tiles, 0, 1 + (i - n_ctx_tiles) // (DEC_SEQ // tm))


def _mod_kernel(c_ref, w_ref, b_ref, o_ref):
    c = c_ref[...]
    s = (c * jax.nn.sigmoid(c)).astype(BF16)
    o_ref[...] = _dot(s, w_ref[...].astype(BF16)) + b_ref[...]


def _modulation(cvec, w_ada, b_ada):
    tn = 1536
    return pl.pallas_call(
        _mod_kernel,
        grid=(DEPTH, 6 * D_MODEL // tn),
        in_specs=[
            pl.BlockSpec((N_MOD_ROWS, D_MODEL), lambda l, n: (0, 0)),
            pl.BlockSpec((None, D_MODEL, tn), lambda l, n: (l, 0, n)),
            pl.BlockSpec((None, 1, tn), lambda l, n: (l, 0, n)),
        ],
        out_specs=pl.BlockSpec((None, N_MOD_ROWS, tn), lambda l, n: (l, 0, n)),
        out_shape=jax.ShapeDtypeStruct((DEPTH, N_MOD_ROWS, 6 * D_MODEL), F32),
        compiler_params=_cparams(2),
        name="modulation",
    )(cvec, w_ada, b_ada.reshape(DEPTH, 1, 6 * D_MODEL))


def _seg_mean(x2, bd):
    hi = x2.astype(BF16)
    lo = (x2 - hi.astype(F32)).astype(BF16)
    return _dot(hi, bd) + _dot(lo, bd)


def _rope(x, cos, sin_up, sin_dn, shift):
    w = x.shape[-1]
    return x * cos + pltpu.roll(x, w - shift, 1) * sin_up + pltpu.roll(x, shift, 1) * sin_dn


def _dup_heads(x):
    lo = lax.broadcasted_iota(jnp.int32, x.shape, 1) < HD_A
    sw = pltpu.roll(x, HD_A, 1)
    return jnp.where(lo, x, sw).astype(BF16), jnp.where(lo, sw, x).astype(BF16)


def _ones_column(rows):
    return jnp.where(lax.broadcasted_iota(jnp.int32, (rows, LANES), 1) == 0, 1.0, 0.0).astype(BF16)


def _store_mla_kv(y, kr_all, ones, k_ref, v_ref):
    for hd in range(H_B):
        sl = slice(hd * LANES, (hd + 1) * LANES)
        k_ref[hd] = (y[:, sl] + kr_all[:, sl]).astype(BF16)
    for j in range(N_PAIR):
        v_ref[j, :, 0:LANES] = y[:, QB_PAD + j * LANES:QB_PAD + (j + 1) * LANES].astype(BF16)
        v_ref[j, :, LANES:2 * LANES] = ones


def _inproj_kernel(xc_ref, xl_ref, mod_ref, gpre_ref, tab_ref, w_ref, bd_ref, gqa_ref, gka_ref, gckv_ref, wkv_ref, place_ref,
                   qa_ref, qb_ref, ka_ref, va_ref, kdup_ref, vdup_ref, ckv_ref, kr_ref, kb_ref, vb_ref, h_ref):
    x = jnp.where(pl.program_id(0) < N_CTX // TM_IN, xc_ref[...], xl_ref[...])
    sh1 = mod_ref[:, 0:D_MODEL]
    sc1 = mod_ref[:, D_MODEL:2 * D_MODEL]
    h = (_rms(x, gpre_ref[...]) * (1.0 + sc1) + sh1).astype(BF16)
    h_ref[...] = h
    y = _dot(h, w_ref[:, 0:ATT_W])

    cos_a, sa_up, sa_dn = tab_ref[:, 0:128], tab_ref[:, 128:256], tab_ref[:, 256:384]
    cos_b, sb_up, sb_dn = tab_ref[:, 384:512], tab_ref[:, 512:640], tab_ref[:, 640:768]

    q = y[:, OFF_QA:OFF_QA + QA_W]
    q = q * lax.rsqrt(_seg_mean(q * q, bd_ref[...]) + EPS) * gqa_ref[...]
    rep = QA_W // LANES
    q = _rope(q, jnp.tile(cos_a, (1, rep)), jnp.tile(sa_up, (1, rep)), jnp.tile(sa_dn, (1, rep)), HD_A // 4)
    qa_ref[...] = (q * (HD_A ** -0.5)).astype(BF16)

    k = y[:, OFF_KA:OFF_KA + KA_W]
    k = k * lax.rsqrt(_seg_mean(k * k, bd_ref[0:KA_W, 0:KA_W]) + EPS) * gka_ref[...]
    k = _rope(k, cos_a, sa_up, sa_dn, HD_A // 4)
    ka_ref[...] = k
    v = y[:, OFF_VA:OFF_VA + KA_W]
    va_ref[...] = v
    ones = _ones_column(k.shape[0])
    for n, (kd, vd) in enumerate(zip(_dup_heads(k), _dup_heads(v))):
        kdup_ref[n] = kd
        vdup_ref[n, :, 0:LANES] = vd
        vdup_ref[n, :, LANES:2 * LANES] = ones

    qb = y[:, OFF_QB:OFF_QB + QB_PAD]
    qb = _rope(qb, jnp.tile(cos_b, (1, H_B)), jnp.tile(sb_up, (1, H_B)), jnp.tile(sb_dn, (1, H_B)), ROPE_B // 4)
    qb_ref[...] = qb.astype(BF16)

    ckv = _rms(y[:, OFF_CKV:OFF_CKV + KV_RANK], gckv_ref[...])
    ckv_ref[...] = ckv
    kr = _rope(y[:, OFF_KR:OFF_KR + LANES], cos_b, sb_up, sb_dn, ROPE_B // 4)
    kr_ref[...] = kr[:, KR_LANE:KR_LANE + ROPE_B]

    y2 = _dot(ckv.astype(BF16), wkv_ref[...])
    kr_all = _dot(kr.astype(BF16), place_ref[...])
    _store_mla_kv(y2, kr_all, ones, kb_ref, vb_ref)


def _inproj(l, x_ctx, x_lat, mods, gpre, tab, w_att, bd, gqa_t, gka_t, gckv, w_kv, place):
    tm = TM_IN
    n_ctx_tiles = N_CTX // tm
    lat_tiles = DEC_SEQ // tm

    def tab_idx(i):
        return (jnp.where(i < n_ctx_tiles, 0, 1 + (i - n_ctx_tiles) % lat_tiles), 0)

    row = lambda i: (i, 0)
    row3 = lambda i: (0, i, 0)
    const = lambda i: (0, 0)
    layer = lambda i: (l, 0, 0)
    return pl.pallas_call(
        _inproj_kernel,
        grid=(N_TOK // tm,),
        in_specs=[
            pl.BlockSpec((tm, D_MODEL), lambda i: (jnp.minimum(i, n_ctx_tiles - 1), 0)),
            pl.BlockSpec((tm, D_MODEL), lambda i: (jnp.maximum(i - n_ctx_tiles, 0), 0)),
            pl.BlockSpec((None, 1, 6 * D_MODEL), lambda i: (_mod_row(l, i, tm), 0, 0)),
            pl.BlockSpec((None, 1, D_MODEL), layer),
            pl.BlockSpec((tm, TAB_W), tab_idx),
            pl.BlockSpec((None, D_MODEL, ATT_BLK), layer),
            pl.BlockSpec((QA_W, QA_W), const),
            pl.BlockSpec((None, 1, QA_W), layer),
            pl.BlockSpec((None, 1, KA_W), layer),
            pl.BlockSpec((None, 1, KV_RANK), layer),
            pl.BlockSpec((None, KV_RANK, KV_W), layer),
            pl.BlockSpec((LANES, QB_PAD), const),
        ],
        out_specs=[
            pl.BlockSpec((tm, QA_W), row),
            pl.BlockSpec((tm, QB_PAD), row),
            pl.BlockSpec((tm, KA_W), row),
            pl.BlockSpec((tm, KA_W), row),
            pl.BlockSpec((KV_A, tm, LANES), row3),
            pl.BlockSpec((KV_A, tm, 2 * LANES), row3),
            pl.BlockSpec((tm, KV_RANK), row),
            pl.BlockSpec((tm, ROPE_B), row),
            pl.BlockSpec((H_B, tm, LANES), row3),
            pl.BlockSpec((N_PAIR, tm, 2 * LANES), row3),
            pl.BlockSpec((tm, D_MODEL), row),
        ],
        out_shape=[
            jax.ShapeDtypeStruct((N_TOK, QA_W), BF16),
            jax.ShapeDtypeStruct((N_TOK, QB_PAD), BF16),
            jax.ShapeDtypeStruct((N_TOK, KA_W), F32),
            jax.ShapeDtypeStruct((N_TOK, KA_W), F32),
            jax.ShapeDtypeStruct((KV_A, N_TOK, LANES), BF16),
            jax.ShapeDtypeStruct((KV_A, N_TOK, 2 * LANES), BF16),
            jax.ShapeDtypeStruct((N_TOK, KV_RANK), F32),
            jax.ShapeDtypeStruct((N_TOK, ROPE_B), F32),
            jax.ShapeDtypeStruct((H_B, N_TOK, LANES), BF16),
            jax.ShapeDtypeStruct((N_PAIR, N_TOK, 2 * LANES), BF16),
            jax.ShapeDtypeStruct((N_TOK, D_MODEL), BF16),
        ],
        compiler_params=_cparams(1),
        name="inproj",
    )(x_ctx, x_lat, mods, gpre, tab, w_att, bd, gqa_t, gka_t, gckv, w_kv, place)


def _kvup_cache_kernel(c_ref, r_ref, w_ref, place_ref, k_ref, v_ref):
    y = _dot(c_ref[...].astype(BF16), w_ref[...])
    kr_all = _dot(r_ref[...].astype(BF16), place_ref[...])
    _store_mla_kv(y, kr_all, _ones_column(y.shape[0]), k_ref, v_ref)


def _kvup_cache(cache_ckv, cache_kr, w_kv, place32):
    rows = DEC_BATCH * PAST_LEN
    idx = lambda l, b: (l, 0, b, 0)
    return pl.pallas_call(
        _kvup_cache_kernel,
        grid=(DEPTH, DEC_BATCH),
        in_specs=[
            pl.BlockSpec((None, None, PAST_LEN, KV_RANK), lambda l, b: (b, l, 0, 0)),
            pl.BlockSpec((None, None, PAST_LEN, ROPE_B), lambda l, b: (b, l, 0, 0)),
            pl.BlockSpec((None, KV_RANK, KV_W), lambda l, b: (l, 0, 0)),
            pl.BlockSpec((ROPE_B, QB_PAD), lambda l, b: (0, 0)),
        ],
        out_specs=[pl.BlockSpec((None, H_B, PAST_LEN, LANES), idx), pl.BlockSpec((None, N_PAIR, PAST_LEN, 2 * LANES), idx)],
        out_shape=[jax.ShapeDtypeStruct((DEPTH, H_B, rows, LANES), BF16),
                   jax.ShapeDtypeStruct((DEPTH, N_PAIR, rows, 2 * LANES), BF16)],
        compiler_params=_cparams(2),
        name="kvup_cache",
    )(cache_ckv, cache_kr, w_kv, place32)


def _attn_kernel(*refs, n_seg, n_pair, c_exp, mask_q):
    per_pair = 2 + 3 * n_seg
    o_ref = refs[-1]
    lo = lax.broadcasted_iota(jnp.int32, (o_ref.shape[0], LANES), 1) < HD_A

    def scores(q, ks):
        return [lax.dot_general(q, k[...], (((1,), (1,)), ((), ())), preferred_element_type=F32) for k in ks]

    def attend(ss, vs):
        m = functools.reduce(jnp.maximum, [jnp.max(s, axis=-1, keepdims=True) for s in ss]) * c_exp
        full = functools.reduce(jnp.add, [_dot(jnp.exp2(s * c_exp - m).astype(BF16), v[...]) for s, v in zip(ss, vs)])
        return full[:, 0:LANES] / full[:, LANES:LANES + 1]

    for p in range(n_pair):
        r = refs[p * per_pair:(p + 1) * per_pair]
        k_refs, v_refs = r[2:2 + 2 * n_seg], r[2 + 2 * n_seg:]
        qe, qo = r[0][...], r[1][...]
        if mask_q:
            qe = jnp.where(lo, qe, jnp.zeros_like(qe))
            qo = jnp.where(lo, jnp.zeros_like(qo), qo)
        ss_e = scores(qe, k_refs[0::2])
        ss_o = scores(qo, k_refs[1::2])
        o_ref[:, p * LANES:(p + 1) * LANES] = jnp.where(lo, attend(ss_e, v_refs), attend(ss_o, v_refs)).astype(o_ref.dtype)


def _attention(q, segs, *, n_batch, t_len, tok0, qe_lane, qo_lane, scale, mask_q, pairs_per_step, name):
    tq = min(TQ, t_len)
    nq = t_len // tq
    q_blk0 = tok0 // tq
    npp = pairs_per_step

    def q_map(lane_fn, p):
        return lambda b, g, i: (q_blk0 + b * nq + i, lane_fn(g * npp + p))

    def kv_map(f, p):
        return lambda b, g, i: f(b, g * npp + p)

    in_specs, args = [], []
    for p in range(npp):
        in_specs += [pl.BlockSpec((tq, LANES), q_map(qe_lane, p)), pl.BlockSpec((tq, LANES), q_map(qo_lane, p))]
        args += [q, q]
        for k, _, k_blk, _, ke_idx, ko_idx, _ in segs:
            in_specs += [pl.BlockSpec(k_blk, kv_map(ke_idx, p)), pl.BlockSpec(k_blk, kv_map(ko_idx, p))]
            args += [k, k]
        for _, v, _, v_blk, _, _, v_idx in segs:
            in_specs.append(pl.BlockSpec(v_blk, kv_map(v_idx, p)))
            args.append(v)
    return pl.pallas_call(
        functools.partial(_attn_kernel, n_seg=len(segs), n_pair=npp, c_exp=scale * LOG2E, mask_q=mask_q),
        grid=(n_batch, N_PAIR // npp, nq),
        in_specs=in_specs,
        out_specs=pl.BlockSpec((tq, npp * LANES), lambda b, g, i: (b * nq + i, g)),
        out_shape=jax.ShapeDtypeStruct((n_batch * t_len, N_PAIR * LANES), BF16),
        compiler_params=_cparams(3),
        name=name,
    )(*args)


def _rglru_kernel(h_ref, wx_ref, wy_ref, cw_ref, cb_ref, wg_ref, bg_ref, lam_ref, h0_ref,
                  oc_ref, st_ref, af_ref, bf_ref, ab_ref, bb_ref, *, t_len, n_sub):
    n_blk = RNN_CB // RNN_BS
    sub = lax.broadcasted_iota(jnp.int32, (SUBLANES, 1), 0)
    lam = lam_ref[...]
    half_c = (0.5 * RG_C) * (jnp.minimum(lam, 0.0) - jnp.log1p(jnp.exp(-jnp.abs(lam))))

    for s in range(n_sub):
        xr = _dot(h_ref[s * t_len:(s + 1) * t_len, :], wx_ref[...])

        def shifted(shift, keep_first, keep_last):
            x = pltpu.roll(xr, shift % t_len, 0)
            head = jnp.where(keep_first, x[:SUBLANES], 0.0)
            tail = jnp.where(keep_last, x[t_len - SUBLANES:], 0.0)
            return jnp.concatenate([head, x[SUBLANES:t_len - SUBLANES], tail], axis=0)

        x_m2 = shifted(2, sub >= 2, True)
        x_m1 = shifted(1, sub >= 1, True)
        x_p1 = shifted(-1, True, sub < SUBLANES - 1)
        u = cw_ref[0:1, :] * x_m2 + cw_ref[1:2, :] * x_m1 + cw_ref[2:3, :] * xr + cw_ref[3:4, :] * x_p1 + cb_ref[...]
        for j in range(n_blk):
            sl = slice(j * RNN_BS, (j + 1) * RNN_BS)
            uj = u[:, sl]
            g = _dot(uj.astype(BF16), wg_ref[j])
            for d, (a_ref, b_ref) in enumerate(((af_ref, bf_ref), (ab_ref, bb_ref))):
                tr = jnp.tanh(g[:, (2 * d) * RNN_BS:(2 * d + 1) * RNN_BS] + bg_ref[2 * d:2 * d + 1, sl])
                ti = jnp.tanh(g[:, (2 * d + 1) * RNN_BS:(2 * d + 2) * RNN_BS] + bg_ref[2 * d + 1:2 * d + 2, sl])
                c = half_c[d:d + 1, sl]
                a = jnp.exp(c * tr + c)
                a_ref[s * n_blk + j] = a
                om = 1.0 - a * a
                root = jnp.where(om > 0.0, om * lax.rsqrt(om), 0.0)
                b_ref[s * n_blk + j] = root * ((0.5 * ti + 0.5) * uj)

    row = lax.broadcasted_iota(jnp.int32, (SUBLANES, RNN_BS), 0)
    shifts = [k for k in (1, 2, 4) if k < SUBLANES]

    def tile_scan(a, b, entry, reverse):
        for k in shifts:
            if reverse:
                live = row < SUBLANES - k
                a_sh, b_sh = pltpu.roll(a, SUBLANES - k, 0), pltpu.roll(b, SUBLANES - k, 0)
            else:
                live = row >= k
                a_sh, b_sh = pltpu.roll(a, k, 0), pltpu.roll(b, k, 0)
            b = b + a * jnp.where(live, b_sh, 0.0)
            a = a * jnp.where(live, a_sh, 1.0)
        hs = b + a * entry
        last = hs[0:1, :] if reverse else hs[SUBLANES - 1:SUBLANES, :]
        return hs, jnp.broadcast_to(last, hs.shape)

    n_tile = t_len // SUBLANES
    n_chain = n_sub * n_blk

    def step(t, carry):
        fwd = pl.ds(pl.multiple_of(t * SUBLANES, SUBLANES), SUBLANES)
        bwd = pl.ds(pl.multiple_of((n_tile - 1 - t) * SUBLANES, SUBLANES), SUBLANES)
        out = []
        for q in range(n_chain):
            hs, ef = tile_scan(af_ref[q, fwd, :], bf_ref[q, fwd, :], carry[2 * q], False)
            bf_ref[q, fwd, :] = hs
            hs, eb = tile_scan(ab_ref[q, bwd, :], bb_ref[q, bwd, :], carry[2 * q + 1], True)
            bb_ref[q, bwd, :] = hs
            out += [ef, eb]
        return tuple(out)

    init = []
    for s in range(n_sub):
        for j in range(n_blk):
            sl = slice(j * RNN_BS, (j + 1) * RNN_BS)
            init += [jnp.broadcast_to(h0_ref[s, 0:1, sl], (SUBLANES, RNN_BS)),
                     jnp.broadcast_to(h0_ref[s, 1:2, sl], (SUBLANES, RNN_BS))]
    carry = lax.fori_loop(0, n_tile, step, tuple(init), unroll=2 if n_chain <= 2 else 1)

    for s in range(n_sub):
        rows = slice(s * t_len, (s + 1) * t_len)
        yr = _dot(h_ref[rows, :], wy_ref[...])
        for j in range(n_blk):
            sl = slice(j * RNN_BS, (j + 1) * RNN_BS)
            q = s * n_blk + j
            st_ref[s, 0:1, sl] = carry[2 * q][0:1, :]
            st_ref[s, 1:2, sl] = carry[2 * q + 1][0:1, :]
            oc_ref[rows, sl] = (jax.nn.gelu(yr[:, sl]) * (bf_ref[q] + bb_ref[q])).astype(BF16)


def _rglru(l, h_all, w_xy, conv_w, conv_b, wg, bg, lam, h0, *, n_seq, t_len, tok0, n_sub, name):
    cb = RNN_CB
    nb = D_RNN // cb
    rows = n_sub * t_len
    blk0 = tok0 // rows
    chan3 = lambda b, n: (l, 0, n)
    state = pl.BlockSpec((n_sub, 2, cb), lambda b, n: (b, 0, n))
    return pl.pallas_call(
        functools.partial(_rglru_kernel, t_len=t_len, n_sub=n_sub),
        grid=(n_seq // n_sub, nb),
        in_specs=[
            pl.BlockSpec((rows, D_MODEL), lambda b, n: (blk0 + b, 0)),
            pl.BlockSpec((None, D_MODEL, cb), lambda b, n: (l, 0, OFF_XR // cb + n)),
            pl.BlockSpec((None, D_MODEL, cb), lambda b, n: (l, 0, OFF_XR // cb + nb + n)),
            pl.BlockSpec((None, 4, cb), chan3),
            pl.BlockSpec((None, 1, cb), chan3),
            pl.BlockSpec((None, cb // RNN_BS, RNN_BS, 4 * RNN_BS), lambda b, n: (l, n, 0, 0)),
            pl.BlockSpec((None, 4, cb), chan3),
            pl.BlockSpec((None, 2, cb), chan3),
            state,
        ],
        out_specs=[pl.BlockSpec((rows, cb), lambda b, n: (b, n)), state],
        out_shape=[
            jax.ShapeDtypeStruct((n_seq * t_len, D_RNN), BF16),
            jax.ShapeDtypeStruct((n_seq, 2, D_RNN), F32),
        ],
        scratch_shapes=[pltpu.VMEM((n_sub * cb // RNN_BS, t_len, RNN_BS), F32)] * 4,
        compiler_params=_cparams(2),
        name=name,
    )(h_all, w_xy, w_xy, conv_w, conv_b, wg, bg, lam, h0)


def _merge_kernel(oac_ref, oal_ref, obc_ref, obl_ref, occ_ref, ocl_ref, xc_ref, xl_ref, h_ref, mod_ref, gpost_ref, gpre2_ref,
                  woa_ref, wob_ref, woc_ref, wgl_ref, wout_ref, x1_ref, h2_ref):
    h = h_ref[...]
    is_ctx = pl.program_id(0) < N_CTX // TM_MERGE

    def gate(k):
        return jax.nn.sigmoid(_dot(h, wgl_ref[:, k * D_MODEL:(k + 1) * D_MODEL]))

    def branch(c_ref, l_ref, w_ref):
        return _dot(jnp.where(is_ctx, c_ref[...], l_ref[...]), w_ref[...])

    merged = gate(0) * branch(oac_ref, oal_ref, woa_ref)
    merged = merged + gate(1) * branch(obc_ref, obl_ref, wob_ref)
    merged = merged + gate(2) * branch(occ_ref, ocl_ref, woc_ref)
    out = _dot(merged.astype(BF16), wout_ref[...])
    gt1 = mod_ref[:, 2 * D_MODEL:3 * D_MODEL]
    sh2 = mod_ref[:, 3 * D_MODEL:4 * D_MODEL]
    sc2 = mod_ref[:, 4 * D_MODEL:5 * D_MODEL]
    x1 = jnp.where(is_ctx, xc_ref[...], xl_ref[...]) + gt1 * _rms(out, gpost_ref[...])
    x1_ref[...] = x1
    h2_ref[...] = (_rms(x1, gpre2_ref[...]) * (1.0 + sc2) + sh2).astype(BF16)


def _merge(l, oa, ob, oc, x, h, mods, gpost, gpre2, w_oa, w_ob, w_oc, w_gl, w_out):
    tm = TM_MERGE
    nct = N_CTX // tm
    row = lambda i: (i, 0)
    ctx = lambda i: (jnp.minimum(i, nct - 1), 0)
    lat = lambda i: (jnp.maximum(i - nct, 0), 0)
    layer = lambda i: (l, 0, 0)

    def pair(width):
        return [pl.BlockSpec((tm, width), ctx), pl.BlockSpec((tm, width), lat)]

    return pl.pallas_call(
        _merge_kernel,
        grid=(N_TOK // tm,),
        in_specs=pair(QA_W) + pair(OB_W) + pair(D_RNN) + pair(D_MODEL) + [
            pl.BlockSpec((tm, D_MODEL), row),
            pl.BlockSpec((None, 1, 6 * D_MODEL), lambda i: (_mod_row(l, i, tm), 0, 0)),
            pl.BlockSpec((None, 1, D_MODEL), layer),
            pl.BlockSpec((None, 1, D_MODEL), layer),
            pl.BlockSpec((None, QA_W, D_MODEL), layer),
            pl.BlockSpec((None, OB_W, D_MODEL), layer),
            pl.BlockSpec((None, D_RNN, D_MODEL), layer),
            pl.BlockSpec((None, D_MODEL, 3 * D_MODEL), lambda i: (l, 0, OFF_GL // (3 * D_MODEL))),
            pl.BlockSpec((None, D_MODEL, D_MODEL), layer),
        ],
        out_specs=[pl.BlockSpec((tm, D_MODEL), row), pl.BlockSpec((tm, D_MODEL), row)],
        out_shape=[
            jax.ShapeDtypeStruct((N_TOK, D_MODEL), F32),
            jax.ShapeDtypeStruct((N_TOK, D_MODEL), BF16),
        ],
        compiler_params=_cparams(1),
        name="merge",
    )(*oa, *ob, *oc, *x, h, mods, gpost, gpre2, w_oa, w_ob, w_oc, w_gl, w_out)


def _ffn_kernel(hp_ref, hm_ref, hn_ref, x1_ref, mod_ref, gpost_ref, wu_ref, cw_ref, cb_ref, wd_ref, o_ref, hext_ref,
                *, t_len):
    tm = TM_FFN
    if t_len >= tm:
        tiles_per_seq = t_len // tm
        pos = pl.program_id(0) % tiles_per_seq
        hext_ref[0:HALO, :] = jnp.where(pos == 0, jnp.zeros_like(hp_ref[...]), hp_ref[...])
        hext_ref[HALO + tm:, :] = jnp.where(pos == tiles_per_seq - 1, jnp.zeros_like(hn_ref[...]), hn_ref[...])
        inner = []
    else:
        hext_ref[0:HALO, :] = jnp.zeros((HALO, D_MODEL), BF16)
        hext_ref[HALO + tm:, :] = jnp.zeros((HALO, D_MODEL), BF16)
        inner = list(range(t_len, tm, t_len))
    hext_ref[HALO:HALO + tm, :] = hm_ref[...]
    hext = hext_ref[...]
    n_ext = tm + 2 * HALO
    sub = lax.broadcasted_iota(jnp.int32, (SUBLANES, 1), 0)

    def patch(x, row):
        s0 = row // SUBLANES * SUBLANES
        slab = jnp.where(sub == row - s0, 0.0, x[s0:s0 + SUBLANES])
        return jnp.concatenate([x[:s0], slab, x[s0 + SUBLANES:]], axis=0)

    def conv(cols):
        up = _dot(hext, wu_ref[:, cols])
        prev = pltpu.roll(up, 1, 0)[HALO:HALO + tm]
        nxt = pltpu.roll(up, n_ext - 1, 0)[HALO:HALO + tm]
        for r in inner:
            prev = patch(prev, r)
            nxt = patch(nxt, r - 1)
        return cw_ref[0:1, cols] * prev + cw_ref[1:2, cols] * up[HALO:HALO + tm] + cw_ref[2:3, cols] * nxt + cb_ref[:, cols]

    acc = jnp.zeros((tm, D_MODEL), F32)
    for c0, c1 in FFN_CHUNKS:
        val = conv(slice(c0, c1))
        gat = conv(slice(D_FF + c0, D_FF + c1))
        acc = acc + _dot((jax.nn.gelu(gat) * val).astype(BF16), wd_ref[c0:c1, :])
    gt2 = mod_ref[:, 5 * D_MODEL:6 * D_MODEL]
    o_ref[...] = x1_ref[...] + gt2 * _rms(acc, gpost_ref[...])


def _ffn(l, h2, x1, mods, gpost, w_up, conv_w, conv_b, w_down, *, tok0, n_tok, t_len, name):
    tm = TM_FFN
    per = tm // HALO
    n_halo_blocks = N_TOK // HALO
    t0 = tok0 // tm
    row = lambda i: (t0 + i, 0)
    resident = lambda shape: pl.BlockSpec(shape, lambda i: (l, 0, 0), pipeline_mode=pl.Buffered(1))
    return pl.pallas_call(
        functools.partial(_ffn_kernel, t_len=t_len),
        grid=(n_tok // tm,),
        in_specs=[
            pl.BlockSpec((HALO, D_MODEL), lambda i: (jnp.maximum((t0 + i) * per - 1, 0), 0)),
            pl.BlockSpec((tm, D_MODEL), row),
            pl.BlockSpec((HALO, D_MODEL), lambda i: (jnp.minimum((t0 + i + 1) * per, n_halo_blocks - 1), 0)),
            pl.BlockSpec((tm, D_MODEL), row),
            pl.BlockSpec((None, 1, 6 * D_MODEL), lambda i: (_mod_row(l, t0 + i, tm), 0, 0)),
            pl.BlockSpec((None, 1, D_MODEL), lambda i: (l, 0, 0)),
            resident((None, D_MODEL, 2 * D_FF)),
            resident((None, 3, 2 * D_FF)),
            resident((None, 1, 2 * D_FF)),
            resident((None, D_FF, D_MODEL)),
        ],
        out_specs=pl.BlockSpec((tm, D_MODEL), lambda i: (i, 0)),
        out_shape=jax.ShapeDtypeStruct((n_tok, D_MODEL), F32),
        scratch_shapes=[pltpu.VMEM((tm + 2 * HALO, D_MODEL), BF16)],
        compiler_params=_cparams(1),
        name=name,
    )(h2, h2, h2, x1, mods, gpost, w_up, conv_w, conv_b, w_down)


def _rope_tables():
    t = np.arange(DEC_SEQ)
    row = (t // GRID_W).astype(np.float64)[:, None]
    col = (t % GRID_W).astype(np.float64)[:, None]

    def parts(dim):
        n = dim // 4
        inv = ROPE_THETA ** (-np.arange(n, dtype=np.float64) / n)
        ar, ac = row * inv, col * inv
        z = np.zeros_like(ar)
        cos = np.concatenate([np.cos(ar), np.cos(ar), np.cos(ac), np.cos(ac)], axis=-1)
        s_up = np.concatenate([-np.sin(ar), z, -np.sin(ac), z], axis=-1)
        s_dn = np.concatenate([z, np.sin(ar), z, np.sin(ac)], axis=-1)
        return cos, s_up, s_dn

    a = [np.tile(p, (1, LANES // HD_A)) for p in parts(HD_A)]
    pad = lambda p, fill: np.concatenate(
        [np.full((DEC_SEQ, KR_LANE), fill), p, np.full((DEC_SEQ, LANES - KR_LANE - ROPE_B), fill)], axis=-1)
    cb, sbu, sbd = parts(ROPE_B)
    lat = np.concatenate(a + [pad(cb, 1.0), pad(sbu, 0.0), pad(sbd, 0.0)], axis=-1)
    ident_blk = np.concatenate([np.ones((TM_IN, LANES)), np.zeros((TM_IN, 2 * LANES))], axis=-1)
    ident = np.concatenate([ident_blk, ident_blk], axis=-1)
    return jnp.asarray(np.concatenate([ident, lat], axis=0), F32)


def _pack_w_proj(w_in):
    qb = w_in[:, :, 768:1536].reshape(DEPTH, D_MODEL, H_B, NOPE_B + ROPE_B)
    qb = jnp.pad(qb, ((0, 0), (0, 0), (0, 0), (0, LANES - NOPE_B - ROPE_B))).reshape(DEPTH, D_MODEL, QB_PAD)
    kr = jnp.pad(w_in[:, :, 1792:1824], ((0, 0), (0, 0), (KR_LANE, LANES - KR_LANE - ROPE_B)))
    zeros = lambda n: jnp.zeros((DEPTH, D_MODEL, n), w_in.dtype)
    return jnp.concatenate([w_in[:, :, 0:768], qb, w_in[:, :, 1536:1792], kr, zeros(OFF_XR - ATT_W),
                            w_in[:, :, 1824:3872], zeros(OFF_GL - OFF_XR - 2 * D_RNN), w_in[:, :, 3872:6944]],
                           axis=-1).astype(BF16)


def _pack_w_kv(w_uk, w_uv):
    uk = w_uk.reshape(DEPTH, KV_RANK, H_B, NOPE_B)
    k_part = jnp.pad(uk, ((0, 0), (0, 0), (0, 0), (0, LANES - NOPE_B))).reshape(DEPTH, KV_RANK, QB_PAD)
    return jnp.concatenate([k_part, w_uv], axis=-1).astype(BF16)


def _cache_dup_heads(cache, with_ones):
    x = jnp.transpose(cache, (1, 3, 0, 2, 4)).astype(BF16)
    parts = [x, x]
    if with_ones:
        parts.append(jnp.broadcast_to((jnp.arange(LANES) == 0).astype(BF16), x.shape[:-1] + (LANES,)))
    return jnp.concatenate(parts, axis=-1).reshape(DEPTH, KV_A, DEC_BATCH * PAST_LEN, -1)


def kernel(x_prompt, x_sample, c, cache_gqa_k, cache_gqa_v, cache_mla_ckv, cache_mla_krope, state_rglru_fwd, state_rglru_bwd, c_ctx, w_ada, b_ada, g_pre_mix, g_post_mix, g_pre_ffn, g_post_ffn, w_in, g_qa, g_ka, g_ckv, w_uk, w_uv, conv_rnn_w, conv_rnn_b, w_rg, b_rg, w_ig, b_ig, lam, w_oa, w_ob, w_oc, w_out, w_up, conv_ffn_w, conv_ffn_b, w_down):
    x = (x_prompt.reshape(N_CTX, D_MODEL), x_sample.reshape(N_LAT, D_MODEL))
    cvec = jnp.concatenate([c_ctx[None, :], c, jnp.zeros((N_MOD_ROWS - 1 - DEC_BATCH, D_MODEL), F32)], axis=0)
    mods = _modulation(cvec, w_ada, b_ada).reshape(DEPTH * N_MOD_ROWS, 1, 6 * D_MODEL)

    vec = lambda g: g.reshape(DEPTH, 1, -1)
    tab = _rope_tables()
    seg = np.arange(QA_W) // HD_A
    bd = jnp.asarray(np.where(seg[:, None] == seg[None, :], 1.0 / HD_A, 0.0), BF16)
    lane = np.arange(QB_PAD) % LANES
    place32_np = (lane[None, :] == KR_LANE + np.arange(ROPE_B)[:, None]).astype(np.float32)
    place32 = jnp.asarray(place32_np, BF16)
    place = jnp.asarray(np.pad(place32_np, ((KR_LANE, LANES - KR_LANE - ROPE_B), (0, 0))), BF16)
    w_att = w_xy = w_gl = _pack_w_proj(w_in)
    w_kv = _pack_w_kv(w_uk, w_uv)
    gqa_t = jnp.tile(g_qa, (1, H_A)).reshape(DEPTH, 1, QA_W)
    gka_t = jnp.tile(g_ka, (1, KV_A)).reshape(DEPTH, 1, KA_W)
    wg = (0.5 * jnp.concatenate([w_rg[:, 0], w_ig[:, 0], w_rg[:, 1], w_ig[:, 1]], axis=-1)).astype(BF16)
    bg = 0.5 * jnp.stack([b_rg[:, 0], b_ig[:, 0], b_rg[:, 1], b_ig[:, 1]], axis=1)
    h0_lat = jnp.stack([state_rglru_fwd, state_rglru_bwd], axis=0)
    h0_zero = jnp.zeros((BATCH, 2, D_RNN), F32)
    w_oa_b, w_ob_b, w_oc_b, w_out_b = (w.astype(BF16) for w in (w_oa, w_ob, w_oc, w_out))
    w_up_b, w_down_b = w_up.astype(BF16), w_down.astype(BF16)
    conv_ffn_b3 = vec(conv_ffn_b)

    kc_a = _cache_dup_heads(cache_gqa_k, False)
    vc_a = _cache_dup_heads(cache_gqa_v, True)
    kc_b, vc_b = _kvup_cache(cache_mla_ckv, cache_mla_krope, w_kv, place32)

    lat_blk0 = N_CTX // DEC_SEQ
    gqa_k = gqa_v = lambda j: j // 2
    mla_ke = lambda j: 2 * j
    mla_ko = lambda j: 2 * j + 1
    mla_v = lambda j: j

    def new_seg(k, v, rows, blk0, fe, fo, fv):
        at = lambda f: lambda b, j: (f(j), blk0 + b, 0)
        return (k, v, (None, rows, LANES), (None, rows, 2 * LANES), at(fe), at(fo), at(fv))

    new_k, new_v, new_ckv, new_kr, new_fwd, new_bwd = [], [], [], [], [], []
    for l in range(DEPTH):
        qa, qb, ka, va, kdup, vdup, ckv, kr, kb, vb, h = _inproj(
            l, *x, mods, vec(g_pre_mix), tab, w_att, bd, gqa_t, gka_t, vec(g_ckv), w_kv, place)

        def cache_seg(k, v, fe, fo, fv):
            at = lambda f: lambda b, j: (l, f(j), b, 0)
            return (k, v, (None, None, PAST_LEN, LANES), (None, None, PAST_LEN, 2 * LANES), at(fe), at(fo), at(fv))

        gqa_idx = (gqa_k, gqa_k, gqa_v)
        gqa = dict(qe_lane=lambda j: j, qo_lane=lambda j: j, scale=1.0, mask_q=True)
        oa_ctx = _attention(qa, [new_seg(kdup, vdup, SEQ, 0, *gqa_idx)],
                            n_batch=BATCH, t_len=SEQ, tok0=0, pairs_per_step=N_PAIR, name="gqa_ctx", **gqa)
        oa_lat = _attention(qa, [cache_seg(kc_a, vc_a, *gqa_idx), new_seg(kdup, vdup, DEC_SEQ, lat_blk0, *gqa_idx)],
                            n_batch=DEC_BATCH, t_len=DEC_SEQ, tok0=N_CTX, pairs_per_step=1, name="gqa_lat", **gqa)
        mla_idx = (mla_ke, mla_ko, mla_v)
        mla = dict(qe_lane=mla_ke, qo_lane=mla_ko, scale=(NOPE_B + ROPE_B) ** -0.5, mask_q=False)
        ob_ctx = _attention(qb, [new_seg(kb, vb, SEQ, 0, *mla_idx)],
                            n_batch=BATCH, t_len=SEQ, tok0=0, pairs_per_step=N_PAIR, name="mla_ctx", **mla)
        ob_lat = _attention(qb, [cache_seg(kc_b, vc_b, *mla_idx), new_seg(kb, vb, DEC_SEQ, lat_blk0, *mla_idx)],
                            n_batch=DEC_BATCH, t_len=DEC_SEQ, tok0=N_CTX, pairs_per_step=1, name="mla_lat", **mla)

        rnn_args = (w_xy, conv_rnn_w, vec(conv_rnn_b), wg, bg, lam)
        oc_ctx, st_ctx = _rglru(l, h, *rnn_args, h0_zero,
                                n_seq=BATCH, t_len=SEQ, tok0=0, n_sub=RNN_CTX_SUB, name="rglru_ctx")
        oc_lat, _ = _rglru(l, h, *rnn_args, jnp.moveaxis(h0_lat[:, :, l], 0, 1),
                           n_seq=DEC_BATCH, t_len=DEC_SEQ, tok0=N_CTX, n_sub=1, name="rglru_lat")

        x1, h2 = _merge(l, (oa_ctx, oa_lat), (ob_ctx, ob_lat), (oc_ctx, oc_lat), x, h, mods,
                        vec(g_post_mix), vec(g_pre_ffn), w_oa_b, w_ob_b, w_oc_b, w_gl, w_out_b)
        ffn_args = (l, h2, x1, mods, vec(g_post_ffn), w_up_b, conv_ffn_w, conv_ffn_b3, w_down_b)
        x = (_ffn(*ffn_args, tok0=0, n_tok=N_CTX, t_len=SEQ, name="ffn_ctx"),
             _ffn(*ffn_args, tok0=N_CTX, n_tok=N_LAT, t_len=DEC_SEQ, name="ffn_lat"))

        new_k.append(ka[:N_CTX].reshape(BATCH, SEQ, KV_A, HD_A))
        new_v.append(va[:N_CTX].reshape(BATCH, SEQ, KV_A, HD_A))
        new_ckv.append(ckv[:N_CTX].reshape(BATCH, SEQ, KV_RANK))
        new_kr.append(kr[:N_CTX].reshape(BATCH, SEQ, ROPE_B))
        new_fwd.append(st_ctx[:, 0])
        new_bwd.append(st_ctx[:, 1])

    stack = lambda xs: jnp.stack(xs, axis=1)
    return (x[0].reshape(BATCH, SEQ, D_MODEL), x[1].reshape(DEC_BATCH, DEC_SEQ, D_MODEL),
            stack(new_k), stack(new_v), stack(new_ckv), stack(new_kr), stack(new_fwd), stack(new_bwd))
```

```python
import functools
import math

import jax
import jax.numpy as jnp
import numpy as np
from jax import lax
from jax.experimental import pallas as pl
from jax.experimental.pallas import tpu as pltpu

F32 = jnp.float32
BF16 = jnp.bfloat16

D_MODEL = 1024
BATCH = 16
SEQ = 256
DEPTH = 2
DEC_BATCH = 4
DEC_SEQ = 2048
PAST_LEN = 512
GRID_W = 64
H_A = 8
KV_A = 2
HD_A = 64
H_B = 8
NOPE_B = 64
ROPE_B = 32
VD_B = 64
KV_RANK = 256
D_RNN = 1024
RNN_BLOCKS = 8
RNN_BS = D_RNN // RNN_BLOCKS
RG_C = 8.0
D_FF = 2816
ROPE_THETA = 10000.0
EPS = 1e-6
QA_W = H_A * HD_A
KA_W = KV_A * HD_A
QB_W = H_B * (NOPE_B + ROPE_B)
OB_W = H_B * VD_B

LANES = 128
SUBLANES = 8
N_CTX = BATCH * SEQ
N_LAT = DEC_BATCH * DEC_SEQ
N_TOK = N_CTX + N_LAT
N_MOD_ROWS = 8
QB_PAD = H_B * LANES
KR_LANE = NOPE_B
N_PAIR = 4
OFF_QA, OFF_KA, OFF_VA, OFF_QB, OFF_CKV, OFF_KR = 0, 512, 640, 768, 1792, 2048
ATT_W = OFF_KR + LANES
ATT_BLK = 2304
OFF_XR = ATT_BLK
OFF_GL = 6144
PROJ_W = OFF_GL + 3 * D_MODEL
KV_W = QB_PAD + OB_W
TAB_W = 6 * LANES
LOG2E = math.log2(math.e)

TM_IN = 256
TQ = 512
RNN_CB = 256
RNN_CTX_SUB = 4
TM_MERGE = 512
TM_FFN = 512
FFN_CHUNKS = ((0, 1536), (1536, D_FF))
HALO = 16
VMEM_LIMIT = 56 * 1024 * 1024


def _cparams(n_axes):
    return pltpu.CompilerParams(dimension_semantics=("arbitrary",) * n_axes, vmem_limit_bytes=VMEM_LIMIT)


def _dot(a, b):
    return jnp.dot(a, b, preferred_element_type=F32)


def _rms(x, g):
    return x * lax.rsqrt(jnp.mean(x * x, axis=-1, keepdims=True) + EPS) * g


def _mod_row(l, i, tm):
    n_ctx_tiles = N_CTX // tm
    return l * N_MOD_ROWS + jnp.where(i < n_ctx_tiles, 0, 1 + (i - n_ctx_tiles) // (DEC_SEQ // tm))


def _mod_kernel(c_ref, w_ref, b_ref, o_ref):
    c = c_ref[...]
    s = (c * jax.nn.sigmoid(c)).astype(BF16)
    o_ref[...] = _dot(s, w_ref[...].astype(BF16)) + b_ref[...]


def _modulation(cvec, w_ada, b_ada):
    tn = 1536
    return pl.pallas_call(
        _mod_kernel,
        grid=(DEPTH, 6 * D_MODEL // tn),
        in_specs=[
            pl.BlockSpec((N_MOD_ROWS, D_MODEL), lambda l, n: (0, 0)),
            pl.BlockSpec((None, D_MODEL, tn), lambda l, n: (l, 0, n)),
            pl.BlockSpec((None, 1, tn), lambda l, n: (l, 0, n)),
        ],
        out_specs=pl.BlockSpec((None, N_MOD_ROWS, tn), lambda l, n: (l, 0, n)),
        out_shape=jax.ShapeDtypeStruct((DEPTH, N_MOD_ROWS, 6 * D_MODEL), F32),
        compiler_params=_cparams(2),
        name="modulation",
    )(cvec, w_ada, b_ada.reshape(DEPTH, 1, 6 * D_MODEL))


def _seg_mean(x2, bd):
    hi = x2.astype(BF16)
    lo = (x2 - hi.astype(F32)).astype(BF16)
    return _dot(hi, bd) + _dot(lo, bd)


def _rope(x, cos, sin_up, sin_dn, shift):
    w = x.shape[-1]
    return x * cos + pltpu.roll(x, w - shift, 1) * sin_up + pltpu.roll(x, shift, 1) * sin_dn


def _dup_heads(x):
    lo = lax.broadcasted_iota(jnp.int32, x.shape, 1) < HD_A
    sw = pltpu.roll(x, HD_A, 1)
    return jnp.where(lo, x, sw).astype(BF16), jnp.where(lo, sw, x).astype(BF16)


def _ones_column(rows):
    return jnp.where(lax.broadcasted_iota(jnp.int32, (rows, LANES), 1) == 0, 1.0, 0.0).astype(BF16)


def _store_mla_kv(y, kr_all, ones, k_ref, v_ref):
    for hd in range(H_B):
        sl = slice(hd * LANES, (hd + 1) * LANES)
        k_ref[hd] = (y[:, sl] + kr_all[:, sl]).astype(BF16)
    for j in range(N_PAIR):
        v_ref[j, :, 0:LANES] = y[:, QB_PAD + j * LANES:QB_PAD + (j + 1) * LANES].astype(BF16)
        v_ref[j, :, LANES:2 * LANES] = ones


def _inproj_kernel(xc_ref, xl_ref, mod_ref, gpre_ref, tab_ref, w_ref, bd_ref, gqa_ref, gka_ref, gckv_ref, wkv_ref, place_ref,
                   qa_ref, qb_ref, ka_ref, va_ref, kdup_ref, vdup_ref, ckv_ref, kr_ref, kb_ref, vb_ref, h_ref):
    x = jnp.where(pl.program_id(0) < N_CTX // TM_IN, xc_ref[...], xl_ref[...])
    sh1 = mod_ref[:, 0:D_MODEL]
    sc1 = mod_ref[:, D_MODEL:2 * D_MODEL]
    h = (_rms(x, gpre_ref[...]) * (1.0 + sc1) + sh1).astype(BF16)
    h_ref[...] = h
    y = _dot(h, w_ref[:, 0:ATT_W])

    cos_a, sa_up, sa_dn = tab_ref[:, 0:128], tab_ref[:, 128:256], tab_ref[:, 256:384]
    cos_b, sb_up, sb_dn = tab_ref[:, 384:512], tab_ref[:, 512:640], tab_ref[:, 640:768]

    q = y[:, OFF_QA:OFF_QA + QA_W]
    q = q * lax.rsqrt(_seg_mean(q * q, bd_ref[...]) + EPS) * gqa_ref[...]
    rep = QA_W // LANES
    q = _rope(q, jnp.tile(cos_a, (1, rep)), jnp.tile(sa_up, (1, rep)), jnp.tile(sa_dn, (1, rep)), HD_A // 4)
    qa_ref[...] = (q * (HD_A ** -0.5 * LOG2E)).astype(BF16)

    k = y[:, OFF_KA:OFF_KA + KA_W]
    k = k * lax.rsqrt(_seg_mean(k * k, bd_ref[0:KA_W, 0:KA_W]) + EPS) * gka_ref[...]
    k = _rope(k, cos_a, sa_up, sa_dn, HD_A // 4)
    ka_ref[...] = k
    v = y[:, OFF_VA:OFF_VA + KA_W]
    va_ref[...] = v
    ones = _ones_column(k.shape[0])
    for n, (kd, vd) in enumerate(zip(_dup_heads(k), _dup_heads(v))):
        kdup_ref[n] = kd
        vdup_ref[n, :, 0:LANES] = vd
        vdup_ref[n, :, LANES:2 * LANES] = ones

    qb = y[:, OFF_QB:OFF_QB + QB_PAD]
    qb = _rope(qb, jnp.tile(cos_b, (1, H_B)), jnp.tile(sb_up, (1, H_B)), jnp.tile(sb_dn, (1, H_B)), ROPE_B // 4)
    qb_ref[...] = (qb * ((NOPE_B + ROPE_B) ** -0.5 * LOG2E)).astype(BF16)

    ckv = _rms(y[:, OFF_CKV:OFF_CKV + KV_RANK], gckv_ref[...])
    ckv_ref[...] = ckv
    kr = _rope(y[:, OFF_KR:OFF_KR + LANES], cos_b, sb_up, sb_dn, ROPE_B // 4)
    kr_ref[...] = kr[:, KR_LANE:KR_LANE + ROPE_B]

    y2 = _dot(ckv.astype(BF16), wkv_ref[...])
    kr_all = _dot(kr.astype(BF16), place_ref[...])
    _store_mla_kv(y2, kr_all, ones, kb_ref, vb_ref)


def _inproj(l, x_ctx, x_lat, mods, gpre, tab, w_att, bd, gqa_t, gka_t, gckv, w_kv, place):
    tm = TM_IN
    n_ctx_tiles = N_CTX // tm
    lat_tiles = DEC_SEQ // tm

    def tab_idx(i):
        return (jnp.where(i < n_ctx_tiles, 0, 1 + (i - n_ctx_tiles) % lat_tiles), 0)

    row = lambda i: (i, 0)
    row3 = lambda i: (0, i, 0)
    const = lambda i: (0, 0)
    layer = lambda i: (l, 0, 0)
    return pl.pallas_call(
        _inproj_kernel,
        grid=(N_TOK // tm,),
        in_specs=[
            pl.BlockSpec((tm, D_MODEL), lambda i: (jnp.minimum(i, n_ctx_tiles - 1), 0)),
            pl.BlockSpec((tm, D_MODEL), lambda i: (jnp.maximum(i - n_ctx_tiles, 0), 0)),
            pl.BlockSpec((None, 1, 6 * D_MODEL), lambda i: (_mod_row(l, i, tm), 0, 0)),
            pl.BlockSpec((None, 1, D_MODEL), layer),
            pl.BlockSpec((tm, TAB_W), tab_idx),
            pl.BlockSpec((None, D_MODEL, ATT_BLK), layer),
            pl.BlockSpec((QA_W, QA_W), const),
            pl.BlockSpec((None, 1, QA_W), layer),
            pl.BlockSpec((None, 1, KA_W), layer),
            pl.BlockSpec((None, 1, KV_RANK), layer),
            pl.BlockSpec((None, KV_RANK, KV_W), layer),
            pl.BlockSpec((LANES, QB_PAD), const),
        ],
        out_specs=[
            pl.BlockSpec((tm, QA_W), row),
            pl.BlockSpec((tm, QB_PAD), row),
            pl.BlockSpec((tm, KA_W), row),
            pl.BlockSpec((tm, KA_W), row),
            pl.BlockSpec((KV_A, tm, LANES), row3),
            pl.BlockSpec((KV_A, tm, 2 * LANES), row3),
            pl.BlockSpec((tm, KV_RANK), row),
            pl.BlockSpec((tm, ROPE_B), row),
            pl.BlockSpec((H_B, tm, LANES), row3),
            pl.BlockSpec((N_PAIR, tm, 2 * LANES), row3),
            pl.BlockSpec((tm, D_MODEL), row),
        ],
        out_shape=[
            jax.ShapeDtypeStruct((N_TOK, QA_W), BF16),
            jax.ShapeDtypeStruct((N_TOK, QB_PAD), BF16),
            jax.ShapeDtypeStruct((N_TOK, KA_W), F32),
            jax.ShapeDtypeStruct((N_TOK, KA_W), F32),
            jax.ShapeDtypeStruct((KV_A, N_TOK, LANES), BF16),
            jax.ShapeDtypeStruct((KV_A, N_TOK, 2 * LANES), BF16),
            jax.ShapeDtypeStruct((N_TOK, KV_RANK), F32),
            jax.ShapeDtypeStruct((N_TOK, ROPE_B), F32),
            jax.ShapeDtypeStruct((H_B, N_TOK, LANES), BF16),
            jax.ShapeDtypeStruct((N_PAIR, N_TOK, 2 * LANES), BF16),
            jax.ShapeDtypeStruct((N_TOK, D_MODEL), BF16),
        ],
        compiler_params=_cparams(1),
        name="inproj",
    )(x_ctx, x_lat, mods, gpre, tab, w_att, bd, gqa_t, gka_t, gckv, w_kv, place)


def _kvup_cache_kernel(c_ref, r_ref, w_ref, place_ref, k_ref, v_ref):
    y = _dot(c_ref[...].astype(BF16), w_ref[...])
    kr_all = _dot(r_ref[...].astype(BF16), place_ref[...])
    _store_mla_kv(y, kr_all, _ones_column(y.shape[0]), k_ref, v_ref)


def _kvup_cache(cache_ckv, cache_kr, w_kv, place32):
    rows = DEC_BATCH * PAST_LEN
    idx = lambda l, b: (l, 0, b, 0)
    return pl.pallas_call(
        _kvup_cache_kernel,
        grid=(DEPTH, DEC_BATCH),
        in_specs=[
            pl.BlockSpec((None, None, PAST_LEN, KV_RANK), lambda l, b: (b, l, 0, 0)),
            pl.BlockSpec((None, None, PAST_LEN, ROPE_B), lambda l, b: (b, l, 0, 0)),
            pl.BlockSpec((None, KV_RANK, KV_W), lambda l, b: (l, 0, 0)),
            pl.BlockSpec((ROPE_B, QB_PAD), lambda l, b: (0, 0)),
        ],
        out_specs=[pl.BlockSpec((None, H_B, PAST_LEN, LANES), idx), pl.BlockSpec((None, N_PAIR, PAST_LEN, 2 * LANES), idx)],
        out_shape=[jax.ShapeDtypeStruct((DEPTH, H_B, rows, LANES), BF16),
                   jax.ShapeDtypeStruct((DEPTH, N_PAIR, rows, 2 * LANES), BF16)],
        compiler_params=_cparams(2),
        name="kvup_cache",
    )(cache_ckv, cache_kr, w_kv, place32)


def _attn_kernel(*refs, n_seg, n_pair, mask_q):
    per_pair = 2 + 3 * n_seg
    o_ref = refs[-1]
    lo = lax.broadcasted_iota(jnp.int32, (o_ref.shape[0], LANES), 1) < HD_A

    def scores(q, ks):
        return [lax.dot_general(q, k[...], (((1,), (1,)), ((), ())), preferred_element_type=F32) for k in ks]

    def attend(ss, vs):
        m = functools.reduce(jnp.maximum, [jnp.max(s, axis=-1, keepdims=True) for s in ss])
        full = functools.reduce(jnp.add, [_dot(jnp.exp2(s - m).astype(BF16), v[...]) for s, v in zip(ss, vs)])
        return full[:, 0:LANES] / full[:, LANES:LANES + 1]

    for p in range(n_pair):
        r = refs[p * per_pair:(p + 1) * per_pair]
        k_refs, v_refs = r[2:2 + 2 * n_seg], r[2 + 2 * n_seg:]
        qe, qo = r[0][...], r[1][...]
        if mask_q:
            qe = jnp.where(lo, qe, jnp.zeros_like(qe))
            qo = jnp.where(lo, jnp.zeros_like(qo), qo)
        ss_e = scores(qe, k_refs[0::2])
        ss_o = scores(qo, k_refs[1::2])
        o_ref[:, p * LANES:(p + 1) * LANES] = jnp.where(lo, attend(ss_e, v_refs), attend(ss_o, v_refs)).astype(o_ref.dtype)


def _attention(q, segs, *, n_batch, t_len, tok0, qe_lane, qo_lane, mask_q, pairs_per_step, name):
    tq = min(TQ, t_len)
    nq = t_len // tq
    q_blk0 = tok0 // tq
    npp = pairs_per_step

    def q_map(lane_fn, p):
        return lambda b, g, i: (q_blk0 + b * nq + i, lane_fn(g * npp + p))

    def kv_map(f, p):
        return lambda b, g, i: f(b, g * npp + p)

    in_specs, args = [], []
    for p in range(npp):
        in_specs += [pl.BlockSpec((tq, LANES), q_map(qe_lane, p)), pl.BlockSpec((tq, LANES), q_map(qo_lane, p))]
        args += [q, q]
        for k, _, k_blk, _, ke_idx, ko_idx, _ in segs:
            in_specs += [pl.BlockSpec(k_blk, kv_map(ke_idx, p)), pl.BlockSpec(k_blk, kv_map(ko_idx, p))]
            args += [k, k]
        for _, v, _, v_blk, _, _, v_idx in segs:
            in_specs.append(pl.BlockSpec(v_blk, kv_map(v_idx, p)))
            args.append(v)
    return pl.pallas_call(
        functools.partial(_attn_kernel, n_seg=len(segs), n_pair=npp, mask_q=mask_q),
        grid=(n_batch, N_PAIR // npp, nq),
        in_specs=in_specs,
        out_specs=pl.BlockSpec((tq, npp * LANES), lambda b, g, i: (b * nq + i, g)),
        out_shape=jax.ShapeDtypeStruct((n_batch * t_len, N_PAIR * LANES), BF16),
        compiler_params=_cparams(3),
        name=name,
    )(*args)


def _rglru_kernel(h_ref, wx_ref, wy_ref, cw_ref, cb_ref, wg_ref, bg_ref, lam_ref, h0_ref,
                  oc_ref, st_ref, af_ref, bf_ref, ab_ref, bb_ref, *, t_len, n_sub):
    n_blk = RNN_CB // RNN_BS
    sub = lax.broadcasted_iota(jnp.int32, (SUBLANES, 1), 0)
    lam = lam_ref[...]
    half_c = (0.5 * RG_C) * (jnp.minimum(lam, 0.0) - jnp.log1p(jnp.exp(-jnp.abs(lam))))

    for s in range(n_sub):
        xr = _dot(h_ref[s * t_len:(s + 1) * t_len, :], wx_ref[...])

        def shifted(shift, keep_first, keep_last):
            x = pltpu.roll(xr, shift % t_len, 0)
            head = jnp.where(keep_first, x[:SUBLANES], 0.0)
            tail = jnp.where(keep_last, x[t_len - SUBLANES:], 0.0)
            return jnp.concatenate([head, x[SUBLANES:t_len - SUBLANES], tail], axis=0)

        x_m2 = shifted(2, sub >= 2, True)
        x_m1 = shifted(1, sub >= 1, True)
        x_p1 = shifted(-1, True, sub < SUBLANES - 1)
        u = cw_ref[0:1, :] * x_m2 + cw_ref[1:2, :] * x_m1 + cw_ref[2:3, :] * xr + cw_ref[3:4, :] * x_p1 + cb_ref[...]
        for j in range(n_blk):
            sl = slice(j * RNN_BS, (j + 1) * RNN_BS)
            uj = u[:, sl]
            half_u = 0.5 * uj
            g = _dot(uj.astype(BF16), wg_ref[j])
            for d, (a_ref, b_ref) in enumerate(((af_ref, bf_ref), (ab_ref, bb_ref))):
                tr = jnp.tanh(g[:, (2 * d) * RNN_BS:(2 * d + 1) * RNN_BS] + bg_ref[2 * d:2 * d + 1, sl])
                ti = jnp.tanh(g[:, (2 * d + 1) * RNN_BS:(2 * d + 2) * RNN_BS] + bg_ref[2 * d + 1:2 * d + 2, sl])
                c = half_c[d:d + 1, sl]
                a = jnp.exp(c * tr + c)
                a_ref[s * n_blk + j] = a
                om = 1.0 - a * a
                root = jnp.where(om > 0.0, om * lax.rsqrt(om), 0.0)
                b_ref[s * n_blk + j] = root * (half_u * ti + half_u)

    row = lax.broadcasted_iota(jnp.int32, (SUBLANES, RNN_BS), 0)
    shifts = [k for k in (1, 2, 4) if k < SUBLANES]

    def tile_scan(a, b, entry, reverse):
        for k in shifts:
            if reverse:
                live = row < SUBLANES - k
                a_sh, b_sh = pltpu.roll(a, SUBLANES - k, 0), pltpu.roll(b, SUBLANES - k, 0)
            else:
                live = row >= k
                a_sh, b_sh = pltpu.roll(a, k, 0), pltpu.roll(b, k, 0)
            b = b + a * jnp.where(live, b_sh, 0.0)
            a = a * jnp.where(live, a_sh, 1.0)
        hs = b + a * entry
        last = hs[0:1, :] if reverse else hs[SUBLANES - 1:SUBLANES, :]
        return hs, jnp.broadcast_to(last, hs.shape)

    n_tile = t_len // SUBLANES
    n_chain = n_sub * n_blk

    def step(t, carry):
        fwd = pl.ds(pl.multiple_of(t * SUBLANES, SUBLANES), SUBLANES)
        bwd = pl.ds(pl.multiple_of((n_tile - 1 - t) * SUBLANES, SUBLANES), SUBLANES)
        out = []
        for q in range(n_chain):
            hs, ef = tile_scan(af_ref[q, fwd, :], bf_ref[q, fwd, :], carry[2 * q], False)
            bf_ref[q, fwd, :] = hs
            hs, eb = tile_scan(ab_ref[q, bwd, :], bb_ref[q, bwd, :], carry[2 * q + 1], True)
            bb_ref[q, bwd, :] = hs
            out += [ef, eb]
        return tuple(out)

    init = []
    for s in range(n_sub):
        for j in range(n_blk):
            sl = slice(j * RNN_BS, (j + 1) * RNN_BS)
            init += [jnp.broadcast_to(h0_ref[s, 0:1, sl], (SUBLANES, RNN_BS)),
                     jnp.broadcast_to(h0_ref[s, 1:2, sl], (SUBLANES, RNN_BS))]
    carry = lax.fori_loop(0, n_tile, step, tuple(init), unroll=2 if n_chain <= 2 else 1)

    for s in range(n_sub):
        rows = slice(s * t_len, (s + 1) * t_len)
        yr = _dot(h_ref[rows, :], wy_ref[...])
        for j in range(n_blk):
            sl = slice(j * RNN_BS, (j + 1) * RNN_BS)
            q = s * n_blk + j
            st_ref[s, 0:1, sl] = carry[2 * q][0:1, :]
            st_ref[s, 1:2, sl] = carry[2 * q + 1][0:1, :]
            oc_ref[rows, sl] = (jax.nn.gelu(yr[:, sl]) * (bf_ref[q] + bb_ref[q])).astype(BF16)


def _rglru(l, h_all, w_xy, conv_w, conv_b, wg, bg, lam, h0, *, n_seq, t_len, tok0, n_sub, name):
    cb = RNN_CB
    nb = D_RNN // cb
    rows = n_sub * t_len
    blk0 = tok0 // rows
    chan3 = lambda b, n: (l, 0, n)
    state = pl.BlockSpec((n_sub, 2, cb), lambda b, n: (b, 0, n))
    return pl.pallas_call(
        functools.partial(_rglru_kernel, t_len=t_len, n_sub=n_sub),
        grid=(n_seq // n_sub, nb),
        in_specs=[
            pl.BlockSpec((rows, D_MODEL), lambda b, n: (blk0 + b, 0)),
            pl.BlockSpec((None, D_MODEL, cb), lambda b, n: (l, 0, OFF_XR // cb + n)),
            pl.BlockSpec((None, D_MODEL, cb), lambda b, n: (l, 0, OFF_XR // cb + nb + n)),
            pl.BlockSpec((None, 4, cb), chan3),
            pl.BlockSpec((None, 1, cb), chan3),
            pl.BlockSpec((None, cb // RNN_BS, RNN_BS, 4 * RNN_BS), lambda b, n: (l, n, 0, 0)),
            pl.BlockSpec((None, 4, cb), chan3),
            pl.BlockSpec((None, 2, cb), chan3),
            state,
        ],
        out_specs=[pl.BlockSpec((rows, cb), lambda b, n: (b, n)), state],
        out_shape=[
            jax.ShapeDtypeStruct((n_seq * t_len, D_RNN), BF16),
            jax.ShapeDtypeStruct((n_seq, 2, D_RNN), F32),
        ],
        scratch_shapes=[pltpu.VMEM((n_sub * cb // RNN_BS, t_len, RNN_BS), F32)] * 4,
        compiler_params=_cparams(2),
        name=name,
    )(h_all, w_xy, w_xy, conv_w, conv_b, wg, bg, lam, h0)


def _merge_kernel(oac_ref, oal_ref, obc_ref, obl_ref, occ_ref, ocl_ref, xc_ref, xl_ref, h_ref, mod_ref, gpost_ref, gpre2_ref,
                  woa_ref, wob_ref, woc_ref, wgl_ref, wout_ref, x1_ref, h2_ref):
    h = h_ref[...]
    is_ctx = pl.program_id(0) < N_CTX // TM_MERGE

    def gate(k):
        return jax.nn.sigmoid(_dot(h, wgl_ref[:, k * D_MODEL:(k + 1) * D_MODEL]))

    def branch(c_ref, l_ref, w_ref):
        return _dot(jnp.where(is_ctx, c_ref[...], l_ref[...]), w_ref[...])

    merged = gate(0) * branch(oac_ref, oal_ref, woa_ref)
    merged = merged + gate(1) * branch(obc_ref, obl_ref, wob_ref)
    merged = merged + gate(2) * branch(occ_ref, ocl_ref, woc_ref)
    out = _dot(merged.astype(BF16), wout_ref[...])
    gt1 = mod_ref[:, 2 * D_MODEL:3 * D_MODEL]
    sh2 = mod_ref[:, 3 * D_MODEL:4 * D_MODEL]
    sc2 = mod_ref[:, 4 * D_MODEL:5 * D_MODEL]
    x1 = jnp.where(is_ctx, xc_ref[...], xl_ref[...]) + gt1 * _rms(out, gpost_ref[...])
    x1_ref[...] = x1
    h2_ref[...] = (_rms(x1, gpre2_ref[...]) * (1.0 + sc2) + sh2).astype(BF16)


def _merge(l, oa, ob, oc, x, h, mods, gpost, gpre2, w_oa, w_ob, w_oc, w_gl, w_out):
    tm = TM_MERGE
    nct = N_CTX // tm
    row = lambda i: (i, 0)
    ctx = lambda i: (jnp.minimum(i, nct - 1), 0)
    lat = lambda i: (jnp.maximum(i - nct, 0), 0)
    layer = lambda i: (l, 0, 0)

    def pair(width):
        return [pl.BlockSpec((tm, width), ctx), pl.BlockSpec((tm, width), lat)]

    return pl.pallas_call(
        _merge_kernel,
        grid=(N_TOK // tm,),
        in_specs=pair(QA_W) + pair(OB_W) + pair(D_RNN) + pair(D_MODEL) + [
            pl.BlockSpec((tm, D_MODEL), row),
            pl.BlockSpec((None, 1, 6 * D_MODEL), lambda i: (_mod_row(l, i, tm), 0, 0)),
            pl.BlockSpec((None, 1, D_MODEL), layer),
            pl.BlockSpec((None, 1, D_MODEL), layer),
            pl.BlockSpec((None, QA_W, D_MODEL), layer),
            pl.BlockSpec((None, OB_W, D_MODEL), layer),
            pl.BlockSpec((None, D_RNN, D_MODEL), layer),
            pl.BlockSpec((None, D_MODEL, 3 * D_MODEL), lambda i: (l, 0, OFF_GL // (3 * D_MODEL))),
            pl.BlockSpec((None, D_MODEL, D_MODEL), layer),
        ],
        out_specs=[pl.BlockSpec((tm, D_MODEL), row), pl.BlockSpec((tm, D_MODEL), row)],
        out_shape=[
            jax.ShapeDtypeStruct((N_TOK, D_MODEL), F32),
            jax.ShapeDtypeStruct((N_TOK, D_MODEL), BF16),
        ],
        compiler_params=_cparams(1),
        name="merge",
    )(*oa, *ob, *oc, *x, h, mods, gpost, gpre2, w_oa, w_ob, w_oc, w_gl, w_out)


def _ffn_kernel(hp_ref, hm_ref, hn_ref, x1_ref, mod_ref, gpost_ref, wu_ref, cw_ref, cb_ref, wd_ref, o_ref, hext_ref,
                *, t_len):
    tm = TM_FFN
    if t_len >= tm:
        tiles_per_seq = t_len // tm
        pos = pl.program_id(0) % tiles_per_seq
        hext_ref[0:HALO, :] = jnp.where(pos == 0, jnp.zeros_like(hp_ref[...]), hp_ref[...])
        hext_ref[HALO + tm:, :] = jnp.where(pos == tiles_per_seq - 1, jnp.zeros_like(hn_ref[...]), hn_ref[...])
        inner = []
    else:
        hext_ref[0:HALO, :] = jnp.zeros((HALO, D_MODEL), BF16)
        hext_ref[HALO + tm:, :] = jnp.zeros((HALO, D_MODEL), BF16)
        inner = list(range(t_len, tm, t_len))
    hext_ref[HALO:HALO + tm, :] = hm_ref[...]
    hext = hext_ref[...]
    n_ext = tm + 2 * HALO
    sub = lax.broadcasted_iota(jnp.int32, (SUBLANES, 1), 0)

    def patch(x, row):
        s0 = row // SUBLANES * SUBLANES
        slab = jnp.where(sub == row - s0, 0.0, x[s0:s0 + SUBLANES])
        return jnp.concatenate([x[:s0], slab, x[s0 + SUBLANES:]], axis=0)

    def conv(cols):
        up = _dot(hext, wu_ref[:, cols])
        prev = pltpu.roll(up, 1, 0)[HALO:HALO + tm]
        nxt = pltpu.roll(up, n_ext - 1, 0)[HALO:HALO + tm]
        for r in inner:
            prev = patch(prev, r)
            nxt = patch(nxt, r - 1)
        return cw_ref[0:1, cols] * prev + cw_ref[1:2, cols] * up[HALO:HALO + tm] + cw_ref[2:3, cols] * nxt + cb_ref[:, cols]

    acc = jnp.zeros((tm, D_MODEL), F32)
    for c0, c1 in FFN_CHUNKS:
        val = conv(slice(c0, c1))
        gat = conv(slice(D_FF + c0, D_FF + c1))
        acc = acc + _dot((jax.nn.gelu(gat) * val).astype(BF16), wd_ref[c0:c1, :])
    gt2 = mod_ref[:, 5 * D_MODEL:6 * D_MODEL]
    o_ref[...] = x1_ref[...] + gt2 * _rms(acc, gpost_ref[...])


def _ffn(l, h2, x1, mods, gpost, w_up, conv_w, conv_b, w_down, *, tok0, n_tok, t_len, name):
    tm = TM_FFN
    per = tm // HALO
    n_halo_blocks = N_TOK // HALO
    t0 = tok0 // tm
    row = lambda i: (t0 + i, 0)
    resident = lambda shape: pl.BlockSpec(shape, lambda i: (l, 0, 0), pipeline_mode=pl.Buffered(1))
    return pl.pallas_call(
        functools.partial(_ffn_kernel, t_len=t_len),
        grid=(n_tok // tm,),
        in_specs=[
            pl.BlockSpec((HALO, D_MODEL), lambda i: (jnp.maximum((t0 + i) * per - 1, 0), 0)),
            pl.BlockSpec((tm, D_MODEL), row),
            pl.BlockSpec((HALO, D_MODEL), lambda i: (jnp.minimum((t0 + i + 1) * per, n_halo_blocks - 1), 0)),
            pl.BlockSpec((tm, D_MODEL), row),
            pl.BlockSpec((None, 1, 6 * D_MODEL), lambda i: (_mod_row(l, t0 + i, tm), 0, 0)),
            pl.BlockSpec((None, 1, D_MODEL), lambda i: (l, 0, 0)),
            resident((None, D_MODEL, 2 * D_FF)),
            resident((None, 3, 2 * D_FF)),
            resident((None, 1, 2 * D_FF)),
            resident((None, D_FF, D_MODEL)),
        ],
        out_specs=pl.BlockSpec((tm, D_MODEL), lambda i: (i, 0)),
        out_shape=jax.ShapeDtypeStruct((n_tok, D_MODEL), F32),
        scratch_shapes=[pltpu.VMEM((tm + 2 * HALO, D_MODEL), BF16)],
        compiler_params=_cparams(1),
        name=name,
    )(h2, h2, h2, x1, mods, gpost, w_up, conv_w, conv_b, w_down)


def _rope_tables():
    t = np.arange(DEC_SEQ)
    row = (t // GRID_W).astype(np.float64)[:, None]
    col = (t % GRID_W).astype(np.float64)[:, None]

    def parts(dim):
        n = dim // 4
        inv = ROPE_THETA ** (-np.arange(n, dtype=np.float64) / n)
        ar, ac = row * inv, col * inv
        z = np.zeros_like(ar)
        cos = np.concatenate([np.cos(ar), np.cos(ar), np.cos(ac), np.cos(ac)], axis=-1)
        s_up = np.concatenate([-np.sin(ar), z, -np.sin(ac), z], axis=-1)
        s_dn = np.concatenate([z, np.sin(ar), z, np.sin(ac)], axis=-1)
        return cos, s_up, s_dn

    a = [np.tile(p, (1, LANES // HD_A)) for p in parts(HD_A)]
    pad = lambda p, fill: np.concatenate(
        [np.full((DEC_SEQ, KR_LANE), fill), p, np.full((DEC_SEQ, LANES - KR_LANE - ROPE_B), fill)], axis=-1)
    cb, sbu, sbd = parts(ROPE_B)
    lat = np.concatenate(a + [pad(cb, 1.0), pad(sbu, 0.0), pad(sbd, 0.0)], axis=-1)
    ident_blk = np.concatenate([np.ones((TM_IN, LANES)), np.zeros((TM_IN, 2 * LANES))], axis=-1)
    ident = np.concatenate([ident_blk, ident_blk], axis=-1)
    return jnp.asarray(np.concatenate([ident, lat], axis=0), F32)


def _pack_proj_kernel(w_ref, o_ref):
    rows = o_ref.shape[0]

    def put(dst, src, n):
        o_ref[:, dst:dst + n] = w_ref[:, src:src + n].astype(BF16)

    def zero(dst, n):
        o_ref[:, dst:dst + n] = jnp.zeros((rows, n), BF16)

    d_qb = NOPE_B + ROPE_B
    put(OFF_QA, 0, OFF_QB)
    for hd in range(H_B):
        put(OFF_QB + hd * LANES, OFF_QB + hd * d_qb, d_qb)
        zero(OFF_QB + hd * LANES + d_qb, LANES - d_qb)
    src_ckv = OFF_QB + QB_W
    put(OFF_CKV, src_ckv, KV_RANK)
    zero(OFF_KR, KR_LANE)
    put(OFF_KR + KR_LANE, src_ckv + KV_RANK, ROPE_B)
    zero(OFF_KR + KR_LANE + ROPE_B, OFF_XR - OFF_KR - KR_LANE - ROPE_B)
    src_xr = src_ckv + KV_RANK + ROPE_B
    put(OFF_XR, src_xr, 2 * D_RNN)
    zero(OFF_XR + 2 * D_RNN, OFF_GL - OFF_XR - 2 * D_RNN)
    put(OFF_GL, src_xr + 2 * D_RNN, 3 * D_MODEL)


def _pack_w_proj(w_in):
    tr = 256
    return pl.pallas_call(
        _pack_proj_kernel,
        grid=(DEPTH, D_MODEL // tr),
        in_specs=[pl.BlockSpec((None, tr, w_in.shape[-1]), lambda l, r: (l, r, 0))],
        out_specs=pl.BlockSpec((None, tr, PROJ_W), lambda l, r: (l, r, 0)),
        out_shape=jax.ShapeDtypeStruct((DEPTH, D_MODEL, PROJ_W), BF16),
        compiler_params=_cparams(2),
        name="pack_proj",
    )(w_in)


def _pack_w_kv(w_uk, w_uv):
    uk = w_uk.reshape(DEPTH, KV_RANK, H_B, NOPE_B)
    k_part = jnp.pad(uk, ((0, 0), (0, 0), (0, 0), (0, LANES - NOPE_B))).reshape(DEPTH, KV_RANK, QB_PAD)
    return jnp.concatenate([k_part, w_uv], axis=-1).astype(BF16)


def _cache_dup_heads(cache, with_ones):
    x = jnp.transpose(cache, (1, 3, 0, 2, 4)).astype(BF16)
    parts = [x, x]
    if with_ones:
        parts.append(jnp.broadcast_to((jnp.arange(LANES) == 0).astype(BF16), x.shape[:-1] + (LANES,)))
    return jnp.concatenate(parts, axis=-1).reshape(DEPTH, KV_A, DEC_BATCH * PAST_LEN, -1)


def kernel(x_prompt, x_sample, c, cache_gqa_k, cache_gqa_v, cache_mla_ckv, cache_mla_krope, state_rglru_fwd, state_rglru_bwd, c_ctx, w_ada, b_ada, g_pre_mix, g_post_mix, g_pre_ffn, g_post_ffn, w_in, g_qa, g_ka, g_ckv, w_uk, w_uv, conv_rnn_w, conv_rnn_b, w_rg, b_rg, w_ig, b_ig, lam, w_oa, w_ob, w_oc, w_out, w_up, conv_ffn_w, conv_ffn_b, w_down):
    x = (x_prompt.reshape(N_CTX, D_MODEL), x_sample.reshape(N_LAT, D_MODEL))
    cvec = jnp.concatenate([c_ctx[None, :], c, jnp.zeros((N_MOD_ROWS - 1 - DEC_BATCH, D_MODEL), F32)], axis=0)
    mods = _modulation(cvec, w_ada, b_ada).reshape(DEPTH * N_MOD_ROWS, 1, 6 * D_MODEL)

    vec = lambda g: g.reshape(DEPTH, 1, -1)
    tab = _rope_tables()
    seg = np.arange(QA_W) // HD_A
    bd = jnp.asarray(np.where(seg[:, None] == seg[None, :], 1.0 / HD_A, 0.0), BF16)
    lane = np.arange(QB_PAD) % LANES
    place32_np = (lane[None, :] == KR_LANE + np.arange(ROPE_B)[:, None]).astype(np.float32)
    place32 = jnp.asarray(place32_np, BF16)
    place = jnp.asarray(np.pad(place32_np, ((KR_LANE, LANES - KR_LANE - ROPE_B), (0, 0))), BF16)
    w_att = w_xy = w_gl = _pack_w_proj(w_in)
    w_kv = _pack_w_kv(w_uk, w_uv)
    gqa_t = jnp.tile(g_qa, (1, H_A)).reshape(DEPTH, 1, QA_W)
    gka_t = jnp.tile(g_ka, (1, KV_A)).reshape(DEPTH, 1, KA_W)
    wg = (0.5 * jnp.concatenate([w_rg[:, 0], w_ig[:, 0], w_rg[:, 1], w_ig[:, 1]], axis=-1)).astype(BF16)
    bg = 0.5 * jnp.stack([b_rg[:, 0], b_ig[:, 0], b_rg[:, 1], b_ig[:, 1]], axis=1)
    h0_lat = jnp.stack([state_rglru_fwd, state_rglru_bwd], axis=0)
    h0_zero = jnp.zeros((BATCH, 2, D_RNN), F32)
    w_oa_b, w_ob_b, w_oc_b, w_out_b = (w.astype(BF16) for w in (w_oa, w_ob, w_oc, w_out))
    w_up_b, w_down_b = w_up.astype(BF16), w_down.astype(BF16)
    conv_ffn_b3 = vec(conv_ffn_b)

    kc_a = _cache_dup_heads(cache_gqa_k, False)
    vc_a = _cache_dup_heads(cache_gqa_v, True)
    kc_b, vc_b = _kvup_cache(cache_mla_ckv, cache_mla_krope, w_kv, place32)

    lat_blk0 = N_CTX // DEC_SEQ
    gqa_k = gqa_v = lambda j: j // 2
    mla_ke = lambda j: 2 * j
    mla_ko = lambda j: 2 * j + 1
    mla_v = lambda j: j

    def new_seg(k, v, rows, blk0, fe, fo, fv):
        at = lambda f: lambda b, j: (f(j), blk0 + b, 0)
        return (k, v, (None, rows, LANES), (None, rows, 2 * LANES), at(fe), at(fo), at(fv))

    new_k, new_v, new_ckv, new_kr, new_fwd, new_bwd = [], [], [], [], [], []
    for l in range(DEPTH):
        qa, qb, ka, va, kdup, vdup, ckv, kr, kb, vb, h = _inproj(
            l, *x, mods, vec(g_pre_mix), tab, w_att, bd, gqa_t, gka_t, vec(g_ckv), w_kv, place)

        def cache_seg(k, v, fe, fo, fv):
            at = lambda f: lambda b, j: (l, f(j), b, 0)
            return (k, v, (None, None, PAST_LEN, LANES), (None, None, PAST_LEN, 2 * LANES), at(fe), at(fo), at(fv))

        gqa_idx = (gqa_k, gqa_k, gqa_v)
        gqa = dict(qe_lane=lambda j: j, qo_lane=lambda j: j, mask_q=True)
        oa_ctx = _attention(qa, [new_seg(kdup, vdup, SEQ, 0, *gqa_idx)],
                            n_batch=BATCH, t_len=SEQ, tok0=0, pairs_per_step=N_PAIR, name="gqa_ctx", **gqa)
        oa_lat = _attention(qa, [cache_seg(kc_a, vc_a, *gqa_idx), new_seg(kdup, vdup, DEC_SEQ, lat_blk0, *gqa_idx)],
                            n_batch=DEC_BATCH, t_len=DEC_SEQ, tok0=N_CTX, pairs_per_step=2, name="gqa_lat", **gqa)
        mla_idx = (mla_ke, mla_ko, mla_v)
        mla = dict(qe_lane=mla_ke, qo_lane=mla_ko, mask_q=False)
        ob_ctx = _attention(qb, [new_seg(kb, vb, SEQ, 0, *mla_idx)],
                            n_batch=BATCH, t_len=SEQ, tok0=0, pairs_per_step=N_PAIR, name="mla_ctx", **mla)
        ob_lat = _attention(qb, [cache_seg(kc_b, vc_b, *mla_idx), new_seg(kb, vb, DEC_SEQ, lat_blk0, *mla_idx)],
                            n_batch=DEC_BATCH, t_len=DEC_SEQ, tok0=N_CTX, pairs_per_step=2, name="mla_lat", **mla)

        rnn_args = (w_xy, conv_rnn_w, vec(conv_rnn_b), wg, bg, lam)
        oc_ctx, st_ctx = _rglru(l, h, *rnn_args, h0_zero,
                                n_seq=BATCH, t_len=SEQ, tok0=0, n_sub=RNN_CTX_SUB, name="rglru_ctx")
        oc_lat, _ = _rglru(l, h, *rnn_args, jnp.moveaxis(h0_lat[:, :, l], 0, 1),
                           n_seq=DEC_BATCH, t_len=DEC_SEQ, tok0=N_CTX, n_sub=1, name="rglru_lat")

        x1, h2 = _merge(l, (oa_ctx, oa_lat), (ob_ctx, ob_lat), (oc_ctx, oc_lat), x, h, mods,
                        vec(g_post_mix), vec(g_pre_ffn), w_oa_b, w_ob_b, w_oc_b, w_gl, w_out_b)
        ffn_args = (l, h2, x1, mods, vec(g_post_ffn), w_up_b, conv_ffn_w, conv_ffn_b3, w_down_b)
        x = (_ffn(*ffn_args, tok0=0, n_tok=N_CTX, t_len=SEQ, name="ffn_ctx"),
             _ffn(*ffn_args, tok0=N_CTX, n_tok=N_LAT, t_len=DEC_SEQ, name="ffn_lat"))

        new_k.append(ka[:N_CTX].reshape(BATCH, SEQ, KV_A, HD_A))
        new_v.append(va[:N_CTX].reshape(BATCH, SEQ, KV_A, HD_A))
        new_ckv.append(ckv[:N_CTX].reshape(BATCH, SEQ, KV_RANK))
        new_kr.append(kr[:N_CTX].reshape(BATCH, SEQ, ROPE_B))
        new_fwd.append(st_ctx[:, 0])
        new_bwd.append(st_ctx[:, 1])

    stack = lambda xs: jnp.stack(xs, axis=1)
    return (x[0].reshape(BATCH, SEQ, D_MODEL), x[1].reshape(DEC_BATCH, DEC_SEQ, D_MODEL),
            stack(new_k), stack(new_v), stack(new_ckv), stack(new_kr), stack(new_fwd), stack(new_bwd))
```

```python
import functools
import math

import jax
import jax.numpy as jnp
import numpy as np
from jax import lax
from jax.experimental import pallas as pl
from jax.experimental.pallas import tpu as pltpu

F32 = jnp.float32
BF16 = jnp.bfloat16

D_MODEL = 1024
BATCH = 16
SEQ = 256
DEPTH = 2
DEC_BATCH = 4
DEC_SEQ = 2048
PAST_LEN = 512
GRID_W = 64
H_A = 8
KV_A = 2
HD_A = 64
H_B = 8
NOPE_B = 64
ROPE_B = 32
VD_B = 64
KV_RANK = 256
D_RNN = 1024
RNN_BLOCKS = 8
RNN_BS = D_RNN // RNN_BLOCKS
RG_C = 8.0
D_FF = 2816
ROPE_THETA = 10000.0
EPS = 1e-6
QA_W = H_A * HD_A
KA_W = KV_A * HD_A
QB_W = H_B * (NOPE_B + ROPE_B)
OB_W = H_B * VD_B

LANES = 128
SUBLANES = 8
N_CTX = BATCH * SEQ
N_LAT = DEC_BATCH * DEC_SEQ
N_TOK = N_CTX + N_LAT
N_MOD_ROWS = 8
QB_PAD = H_B * LANES
KR_LANE = NOPE_B
N_PAIR = 4
OFF_QA, OFF_KA, OFF_VA, OFF_QB, OFF_CKV, OFF_KR = 0, 512, 640, 768, 1792, 2048
ATT_W = OFF_KR + LANES
ATT_BLK = 2304
OFF_XR = ATT_BLK
OFF_GL = 6144
PROJ_W = OFF_GL + 3 * D_MODEL
KV_W = QB_PAD + OB_W
TAB_W = 6 * LANES
LOG2E = math.log2(math.e)

TM_IN = 256
TQ = 512
RNN_CB = 256
RNN_CTX_SUB = 4
TM_MERGE = 512
TM_FFN = 512
FFN_CHUNKS = ((0, 1536), (1536, D_FF))
HALO = 16
VMEM_LIMIT = 56 * 1024 * 1024


def _cparams(n_axes):
    return pltpu.CompilerParams(dimension_semantics=("arbitrary",) * n_axes, vmem_limit_bytes=VMEM_LIMIT)


def _dot(a, b):
    return jnp.dot(a, b, preferred_element_type=F32)


def _rms(x, g):
    return x * lax.rsqrt(jnp.mean(x * x, axis=-1, keepdims=True) + EPS) * g


def _mod_row(l, i, tm):
    n_ctx_tiles = N_CTX // tm
    return l * N_MOD_ROWS + jnp.where(i < n_ctx_tiles, 0, 1 + (i - n_ctx_tiles) // (DEC_SEQ // tm))


def _mod_kernel(c_ref, w_ref, b_ref, o_ref):
    c = c_ref[...]
    s = (c * jax.nn.sigmoid(c)).astype(BF16)
    o_ref[...] = _dot(s, w_ref[...].astype(BF16)) + b_ref[...]


def _modulation(cvec, w_ada, b_ada):
    tn = 1536
    return pl.pallas_call(
        _mod_kernel,
        grid=(DEPTH, 6 * D_MODEL // tn),
        in_specs=[
            pl.BlockSpec((N_MOD_ROWS, D_MODEL), lambda l, n: (0, 0)),
            pl.BlockSpec((None, D_MODEL, tn), lambda l, n: (l, 0, n)),
            pl.BlockSpec((None, 1, tn), lambda l, n: (l, 0, n)),
        ],
        out_specs=pl.BlockSpec((None, N_MOD_ROWS, tn), lambda l, n: (l, 0, n)),
        out_shape=jax.ShapeDtypeStruct((DEPTH, N_MOD_ROWS, 6 * D_MODEL), F32),
        compiler_params=_cparams(2),
        name="modulation",
    )(cvec, w_ada, b_ada.reshape(DEPTH, 1, 6 * D_MODEL))


def _seg_mean(x2, bd):
    hi = x2.astype(BF16)
    lo = (x2 - hi.astype(F32)).astype(BF16)
    return _dot(hi, bd) + _dot(lo, bd)


def _rope(x, cos, sin_up, sin_dn, shift):
    w = x.shape[-1]
    return x * cos + pltpu.roll(x, w - shift, 1) * sin_up + pltpu.roll(x, shift, 1) * sin_dn


def _dup_heads(x):
    lo = lax.broadcasted_iota(jnp.int32, x.shape, 1) < HD_A
    sw = pltpu.roll(x, HD_A, 1)
    return jnp.where(lo, x, sw).astype(BF16), jnp.where(lo, sw, x).astype(BF16)


def _ones_column(rows):
    return jnp.where(lax.broadcasted_iota(jnp.int32, (rows, LANES), 1) == 0, 1.0, 0.0).astype(BF16)


def _store_mla_kv(y, kr_all, ones, k_ref, v_ref):
    for hd in range(H_B):
        sl = slice(hd * LANES, (hd + 1) * LANES)
        k_ref[hd] = (y[:, sl] + kr_all[:, sl]).astype(BF16)
    for j in range(N_PAIR):
        v_ref[j, :, 0:LANES] = y[:, QB_PAD + j * LANES:QB_PAD + (j + 1) * LANES].astype(BF16)
        v_ref[j, :, LANES:2 * LANES] = ones


def _inproj_kernel(xc_ref, xl_ref, mod_ref, gpre_ref, tab_ref, w_ref, bd_ref, gqa_ref, gka_ref, gckv_ref, wkv_ref, place_ref,
                   qa_ref, qb_ref, ka_ref, va_ref, kdup_ref, vdup_ref, ckv_ref, kr_ref, kb_ref, vb_ref, h_ref):
    x = jnp.where(pl.program_id(0) < N_CTX // TM_IN, xc_ref[...], xl_ref[...])
    sh1 = mod_ref[:, 0:D_MODEL]
    sc1 = mod_ref[:, D_MODEL:2 * D_MODEL]
    h = (_rms(x, gpre_ref[...]) * (1.0 + sc1) + sh1).astype(BF16)
    h_ref[...] = h
    y = _dot(h, w_ref[:, 0:ATT_W])

    cos_a, sa_up, sa_dn = tab_ref[:, 0:128], tab_ref[:, 128:256], tab_ref[:, 256:384]
    cos_b, sb_up, sb_dn = tab_ref[:, 384:512], tab_ref[:, 512:640], tab_ref[:, 640:768]

    q = y[:, OFF_QA:OFF_QA + QA_W]
    q = q * lax.rsqrt(_seg_mean(q * q, bd_ref[...]) + EPS) * gqa_ref[...]
    rep = QA_W // LANES
    q = _rope(q, jnp.tile(cos_a, (1, rep)), jnp.tile(sa_up, (1, rep)), jnp.tile(sa_dn, (1, rep)), HD_A // 4)
    qa_ref[...] = (q * (HD_A ** -0.5 * LOG2E)).astype(BF16)

    k = y[:, OFF_KA:OFF_KA + KA_W]
    k = k * lax.rsqrt(_seg_mean(k * k, bd_ref[0:KA_W, 0:KA_W]) + EPS) * gka_ref[...]
    k = _rope(k, cos_a, sa_up, sa_dn, HD_A // 4)
    ka_ref[...] = k
    v = y[:, OFF_VA:OFF_VA + KA_W]
    va_ref[...] = v
    ones = _ones_column(k.shape[0])
    for n, (kd, vd) in enumerate(zip(_dup_heads(k), _dup_heads(v))):
        kdup_ref[n] = kd
        vdup_ref[n, :, 0:LANES] = vd
        vdup_ref[n, :, LANES:2 * LANES] = ones

    qb = y[:, OFF_QB:OFF_QB + QB_PAD]
    qb = _rope(qb, jnp.tile(cos_b, (1, H_B)), jnp.tile(sb_up, (1, H_B)), jnp.tile(sb_dn, (1, H_B)), ROPE_B // 4)
    qb_ref[...] = (qb * ((NOPE_B + ROPE_B) ** -0.5 * LOG2E)).astype(BF16)

    ckv = _rms(y[:, OFF_CKV:OFF_CKV + KV_RANK], gckv_ref[...])
    ckv_ref[...] = ckv
    kr = _rope(y[:, OFF_KR:OFF_KR + LANES], cos_b, sb_up, sb_dn, ROPE_B // 4)
    kr_ref[...] = kr[:, KR_LANE:KR_LANE + ROPE_B]

    y2 = _dot(ckv.astype(BF16), wkv_ref[...])
    kr_all = _dot(kr.astype(BF16), place_ref[...])
    _store_mla_kv(y2, kr_all, ones, kb_ref, vb_ref)


def _inproj(l, x_ctx, x_lat, mods, gpre, tab, w_att, bd, gqa_t, gka_t, gckv, w_kv, place):
    tm = TM_IN
    n_ctx_tiles = N_CTX // tm
    lat_tiles = DEC_SEQ // tm

    def tab_idx(i):
        return (jnp.where(i < n_ctx_tiles, 0, 1 + (i - n_ctx_tiles) % lat_tiles), 0)

    row = lambda i: (i, 0)
    row3 = lambda i: (0, i, 0)
    const = lambda i: (0, 0)
    layer = lambda i: (l, 0, 0)
    return pl.pallas_call(
        _inproj_kernel,
        grid=(N_TOK // tm,),
        in_specs=[
            pl.BlockSpec((tm, D_MODEL), lambda i: (jnp.minimum(i, n_ctx_tiles - 1), 0)),
            pl.BlockSpec((tm, D_MODEL), lambda i: (jnp.maximum(i - n_ctx_tiles, 0), 0)),
            pl.BlockSpec((None, 1, 6 * D_MODEL), lambda i: (_mod_row(l, i, tm), 0, 0)),
            pl.BlockSpec((None, 1, D_MODEL), layer),
            pl.BlockSpec((tm, TAB_W), tab_idx),
            pl.BlockSpec((None, D_MODEL, ATT_BLK), layer),
            pl.BlockSpec((QA_W, QA_W), const),
            pl.BlockSpec((None, 1, QA_W), layer),
            pl.BlockSpec((None, 1, KA_W), layer),
            pl.BlockSpec((None, 1, KV_RANK), layer),
            pl.BlockSpec((None, KV_RANK, KV_W), layer),
            pl.BlockSpec((LANES, QB_PAD), const),
        ],
        out_specs=[
            pl.BlockSpec((tm, QA_W), row),
            pl.BlockSpec((tm, QB_PAD), row),
            pl.BlockSpec((tm, KA_W), row),
            pl.BlockSpec((tm, KA_W), row),
            pl.BlockSpec((KV_A, tm, LANES), row3),
            pl.BlockSpec((KV_A, tm, 2 * LANES), row3),
            pl.BlockSpec((tm, KV_RANK), row),
            pl.BlockSpec((tm, ROPE_B), row),
            pl.BlockSpec((H_B, tm, LANES), row3),
            pl.BlockSpec((N_PAIR, tm, 2 * LANES), row3),
            pl.BlockSpec((tm, D_MODEL), row),
        ],
        out_shape=[
            jax.ShapeDtypeStruct((N_TOK, QA_W), BF16),
            jax.ShapeDtypeStruct((N_TOK, QB_PAD), BF16),
            jax.ShapeDtypeStruct((N_TOK, KA_W), F32),
            jax.ShapeDtypeStruct((N_TOK, KA_W), F32),
            jax.ShapeDtypeStruct((KV_A, N_TOK, LANES), BF16),
            jax.ShapeDtypeStruct((KV_A, N_TOK, 2 * LANES), BF16),
            jax.ShapeDtypeStruct((N_TOK, KV_RANK), F32),
            jax.ShapeDtypeStruct((N_TOK, ROPE_B), F32),
            jax.ShapeDtypeStruct((H_B, N_TOK, LANES), BF16),
            jax.ShapeDtypeStruct((N_PAIR, N_TOK, 2 * LANES), BF16),
            jax.ShapeDtypeStruct((N_TOK, D_MODEL), BF16),
        ],
        compiler_params=_cparams(1),
        name="inproj",
    )(x_ctx, x_lat, mods, gpre, tab, w_att, bd, gqa_t, gka_t, gckv, w_kv, place)


def _kvup_cache_kernel(c_ref, r_ref, w_ref, place_ref, k_ref, v_ref):
    y = _dot(c_ref[...].astype(BF16), w_ref[...])
    kr_all = _dot(r_ref[...].astype(BF16), place_ref[...])
    _store_mla_kv(y, kr_all, _ones_column(y.shape[0]), k_ref, v_ref)


def _kvup_cache(cache_ckv, cache_kr, w_kv, place32):
    rows = DEC_BATCH * PAST_LEN
    idx = lambda l, b: (l, 0, b, 0)
    return pl.pallas_call(
        _kvup_cache_kernel,
        grid=(DEPTH, DEC_BATCH),
        in_specs=[
            pl.BlockSpec((None, None, PAST_LEN, KV_RANK), lambda l, b: (b, l, 0, 0)),
            pl.BlockSpec((None, None, PAST_LEN, ROPE_B), lambda l, b: (b, l, 0, 0)),
            pl.BlockSpec((None, KV_RANK, KV_W), lambda l, b: (l, 0, 0)),
            pl.BlockSpec((ROPE_B, QB_PAD), lambda l, b: (0, 0)),
        ],
        out_specs=[pl.BlockSpec((None, H_B, PAST_LEN, LANES), idx), pl.BlockSpec((None, N_PAIR, PAST_LEN, 2 * LANES), idx)],
        out_shape=[jax.ShapeDtypeStruct((DEPTH, H_B, rows, LANES), BF16),
                   jax.ShapeDtypeStruct((DEPTH, N_PAIR, rows, 2 * LANES), BF16)],
        compiler_params=_cparams(2),
        name="kvup_cache",
    )(cache_ckv, cache_kr, w_kv, place32)


def _attn_kernel(*refs, n_seg, n_pair, mask_q):
    per_pair = 2 + 3 * n_seg
    o_ref = refs[-1]
    lo = lax.broadcasted_iota(jnp.int32, (o_ref.shape[0], LANES), 1) < HD_A

    def scores(q, ks):
        return [lax.dot_general(q, k[...], (((1,), (1,)), ((), ())), preferred_element_type=F32) for k in ks]

    def attend(ss, vs):
        m = functools.reduce(jnp.maximum, [jnp.max(s, axis=-1, keepdims=True) for s in ss])
        full = functools.reduce(jnp.add, [_dot(jnp.exp2(s - m).astype(BF16), v[...]) for s, v in zip(ss, vs)])
        return full[:, 0:LANES] / full[:, LANES:LANES + 1]

    for p in range(n_pair):
        r = refs[p * per_pair:(p + 1) * per_pair]
        k_refs, v_refs = r[2:2 + 2 * n_seg], r[2 + 2 * n_seg:]
        qe, qo = r[0][...], r[1][...]
        if mask_q:
            qe = jnp.where(lo, qe, jnp.zeros_like(qe))
            qo = jnp.where(lo, jnp.zeros_like(qo), qo)
        ss_e = scores(qe, k_refs[0::2])
        ss_o = scores(qo, k_refs[1::2])
        o_ref[:, p * LANES:(p + 1) * LANES] = jnp.where(lo, attend(ss_e, v_refs), attend(ss_o, v_refs)).astype(o_ref.dtype)


def _attention(q, segs, *, n_batch, t_len, tok0, qe_lane, qo_lane, mask_q, pairs_per_step, name):
    tq = min(TQ, t_len)
    nq = t_len // tq
    q_blk0 = tok0 // tq
    npp = pairs_per_step

    def q_map(lane_fn, p):
        return lambda b, g, i: (q_blk0 + b * nq + i, lane_fn(g * npp + p))

    def kv_map(f, p):
        return lambda b, g, i: f(b, g * npp + p)

    in_specs, args = [], []
    for p in range(npp):
        in_specs += [pl.BlockSpec((tq, LANES), q_map(qe_lane, p)), pl.BlockSpec((tq, LANES), q_map(qo_lane, p))]
        args += [q, q]
        for k, _, k_blk, _, ke_idx, ko_idx, _ in segs:
            in_specs += [pl.BlockSpec(k_blk, kv_map(ke_idx, p)), pl.BlockSpec(k_blk, kv_map(ko_idx, p))]
            args += [k, k]
        for _, v, _, v_blk, _, _, v_idx in segs:
            in_specs.append(pl.BlockSpec(v_blk, kv_map(v_idx, p)))
            args.append(v)
    return pl.pallas_call(
        functools.partial(_attn_kernel, n_seg=len(segs), n_pair=npp, mask_q=mask_q),
        grid=(n_batch, N_PAIR // npp, nq),
        in_specs=in_specs,
        out_specs=pl.BlockSpec((tq, npp * LANES), lambda b, g, i: (b * nq + i, g)),
        out_shape=jax.ShapeDtypeStruct((n_batch * t_len, N_PAIR * LANES), BF16),
        compiler_params=_cparams(3),
        name=name,
    )(*args)


def _rglru_kernel(h_ref, wx_ref, wy_ref, cw_ref, cb_ref, wg_ref, bg_ref, lam_ref, h0_ref,
                  oc_ref, st_ref, af_ref, bf_ref, ab_ref, bb_ref, *, t_len, n_sub):
    n_blk = RNN_CB // RNN_BS
    sub = lax.broadcasted_iota(jnp.int32, (SUBLANES, 1), 0)
    lam = lam_ref[...]
    half_c = (0.5 * RG_C) * (jnp.minimum(lam, 0.0) - jnp.log1p(jnp.exp(-jnp.abs(lam))))

    for s in range(n_sub):
        xr = _dot(h_ref[s * t_len:(s + 1) * t_len, :], wx_ref[...])

        def shifted(shift, keep_first, keep_last):
            x = pltpu.roll(xr, shift % t_len, 0)
            head = jnp.where(keep_first, x[:SUBLANES], 0.0)
            tail = jnp.where(keep_last, x[t_len - SUBLANES:], 0.0)
            return jnp.concatenate([head, x[SUBLANES:t_len - SUBLANES], tail], axis=0)

        x_m2 = shifted(2, sub >= 2, True)
        x_m1 = shifted(1, sub >= 1, True)
        x_p1 = shifted(-1, True, sub < SUBLANES - 1)
        u = cw_ref[0:1, :] * x_m2 + cw_ref[1:2, :] * x_m1 + cw_ref[2:3, :] * xr + cw_ref[3:4, :] * x_p1 + cb_ref[...]
        for j in range(n_blk):
            sl = slice(j * RNN_BS, (j + 1) * RNN_BS)
            uj = u[:, sl]
            half_u = 0.5 * uj
            g = _dot(uj.astype(BF16), wg_ref[j])
            for d, (a_ref, b_ref) in enumerate(((af_ref, bf_ref), (ab_ref, bb_ref))):
                tr = jnp.tanh(g[:, (2 * d) * RNN_BS:(2 * d + 1) * RNN_BS] + bg_ref[2 * d:2 * d + 1, sl])
                ti = jnp.tanh(g[:, (2 * d + 1) * RNN_BS:(2 * d + 2) * RNN_BS] + bg_ref[2 * d + 1:2 * d + 2, sl])
                c = half_c[d:d + 1, sl]
                a = jnp.exp(c * tr + c)
                a_ref[s * n_blk + j] = a
                om = 1.0 - a * a
                root = jnp.where(om > 0.0, om * lax.rsqrt(om), 0.0)
                b_ref[s * n_blk + j] = root * (half_u * ti + half_u)

    row = lax.broadcasted_iota(jnp.int32, (SUBLANES, RNN_BS), 0)
    shifts = [k for k in (1, 2, 4) if k < SUBLANES]

    def tile_scan(a, b, entry, reverse):
        for k in shifts:
            if reverse:
                live = row < SUBLANES - k
                a_sh, b_sh = pltpu.roll(a, SUBLANES - k, 0), pltpu.roll(b, SUBLANES - k, 0)
            else:
                live = row >= k
                a_sh, b_sh = pltpu.roll(a, k, 0), pltpu.roll(b, k, 0)
            b = b + a * jnp.where(live, b_sh, 0.0)
            a = a * jnp.where(live, a_sh, 1.0)
        hs = b + a * entry
        last = hs[0:1, :] if reverse else hs[SUBLANES - 1:SUBLANES, :]
        return hs, jnp.broadcast_to(last, hs.shape)

    n_tile = t_len // SUBLANES
    n_chain = n_sub * n_blk

    def step(t, carry):
        fwd = pl.ds(pl.multiple_of(t * SUBLANES, SUBLANES), SUBLANES)
        bwd = pl.ds(pl.multiple_of((n_tile - 1 - t) * SUBLANES, SUBLANES), SUBLANES)
        out = []
        for q in range(n_chain):
            hs, ef = tile_scan(af_ref[q, fwd, :], bf_ref[q, fwd, :], carry[2 * q], False)
            bf_ref[q, fwd, :] = hs
            hs, eb = tile_scan(ab_ref[q, bwd, :], bb_ref[q, bwd, :], carry[2 * q + 1], True)
            bb_ref[q, bwd, :] = hs
            out += [ef, eb]
        return tuple(out)

    init = []
    for s in range(n_sub):
        for j in range(n_blk):
            sl = slice(j * RNN_BS, (j + 1) * RNN_BS)
            init += [jnp.broadcast_to(h0_ref[s, 0:1, sl], (SUBLANES, RNN_BS)),
                     jnp.broadcast_to(h0_ref[s, 1:2, sl], (SUBLANES, RNN_BS))]
    carry = lax.fori_loop(0, n_tile, step, tuple(init), unroll=2 if n_chain <= 2 else 1)

    for s in range(n_sub):
        rows = slice(s * t_len, (s + 1) * t_len)
        yr = _dot(h_ref[rows, :], wy_ref[...])
        for j in range(n_blk):
            sl = slice(j * RNN_BS, (j + 1) * RNN_BS)
            q = s * n_blk + j
            st_ref[s, 0:1, sl] = carry[2 * q][0:1, :]
            st_ref[s, 1:2, sl] = carry[2 * q + 1][0:1, :]
            oc_ref[rows, sl] = (jax.nn.gelu(yr[:, sl]) * (bf_ref[q] + bb_ref[q])).astype(BF16)


def _rglru(l, h_all, w_xy, conv_w, conv_b, wg, bg, lam, h0, *, n_seq, t_len, tok0, n_sub, name):
    cb = RNN_CB
    nb = D_RNN // cb
    rows = n_sub * t_len
    blk0 = tok0 // rows
    chan3 = lambda b, n: (l, 0, n)
    state = pl.BlockSpec((n_sub, 2, cb), lambda b, n: (b, 0, n))
    return pl.pallas_call(
        functools.partial(_rglru_kernel, t_len=t_len, n_sub=n_sub),
        grid=(n_seq // n_sub, nb),
        in_specs=[
            pl.BlockSpec((rows, D_MODEL), lambda b, n: (blk0 + b, 0)),
            pl.BlockSpec((None, D_MODEL, cb), lambda b, n: (l, 0, OFF_XR // cb + n)),
            pl.BlockSpec((None, D_MODEL, cb), lambda b, n: (l, 0, OFF_XR // cb + nb + n)),
            pl.BlockSpec((None, 4, cb), chan3),
            pl.BlockSpec((None, 1, cb), chan3),
            pl.BlockSpec((None, cb // RNN_BS, RNN_BS, 4 * RNN_BS), lambda b, n: (l, n, 0, 0)),
            pl.BlockSpec((None, 4, cb), chan3),
            pl.BlockSpec((None, 2, cb), chan3),
            state,
        ],
        out_specs=[pl.BlockSpec((rows, cb), lambda b, n: (b, n)), state],
        out_shape=[
            jax.ShapeDtypeStruct((n_seq * t_len, D_RNN), BF16),
            jax.ShapeDtypeStruct((n_seq, 2, D_RNN), F32),
        ],
        scratch_shapes=[pltpu.VMEM((n_sub * cb // RNN_BS, t_len, RNN_BS), F32)] * 4,
        compiler_params=_cparams(2),
        name=name,
    )(h_all, w_xy, w_xy, conv_w, conv_b, wg, bg, lam, h0)


def _merge_kernel(oac_ref, oal_ref, obc_ref, obl_ref, occ_ref, ocl_ref, xc_ref, xl_ref, h_ref, mod_ref, gpost_ref, gpre2_ref,
                  woa_ref, wob_ref, woc_ref, wgl_ref, wout_ref, x1_ref, h2_ref):
    h = h_ref[...]
    is_ctx = pl.program_id(0) < N_CTX // TM_MERGE

    def gate(k):
        return jax.nn.sigmoid(_dot(h, wgl_ref[:, k * D_MODEL:(k + 1) * D_MODEL]))

    def branch(c_ref, l_ref, w_ref):
        return _dot(jnp.where(is_ctx, c_ref[...], l_ref[...]), w_ref[...])

    merged = gate(0) * branch(oac_ref, oal_ref, woa_ref)
    merged = merged + gate(1) * branch(obc_ref, obl_ref, wob_ref)
    merged = merged + gate(2) * branch(occ_ref, ocl_ref, woc_ref)
    out = _dot(merged.astype(BF16), wout_ref[...])
    gt1 = mod_ref[:, 2 * D_MODEL:3 * D_MODEL]
    sh2 = mod_ref[:, 3 * D_MODEL:4 * D_MODEL]
    sc2 = mod_ref[:, 4 * D_MODEL:5 * D_MODEL]
    x1 = jnp.where(is_ctx, xc_ref[...], xl_ref[...]) + gt1 * _rms(out, gpost_ref[...])
    x1_ref[...] = x1
    h2_ref[...] = (_rms(x1, gpre2_ref[...]) * (1.0 + sc2) + sh2).astype(BF16)


def _merge(l, oa, ob, oc, x, h, mods, gpost, gpre2, w_oa, w_ob, w_oc, w_gl, w_out):
    tm = TM_MERGE
    nct = N_CTX // tm
    row = lambda i: (i, 0)
    ctx = lambda i: (jnp.minimum(i, nct - 1), 0)
    lat = lambda i: (jnp.maximum(i - nct, 0), 0)
    layer = lambda i: (l, 0, 0)

    def pair(width):
        return [pl.BlockSpec((tm, width), ctx), pl.BlockSpec((tm, width), lat)]

    return pl.pallas_call(
        _merge_kernel,
        grid=(N_TOK // tm,),
        in_specs=pair(QA_W) + pair(OB_W) + pair(D_RNN) + pair(D_MODEL) + [
            pl.BlockSpec((tm, D_MODEL), row),
            pl.BlockSpec((None, 1, 6 * D_MODEL), lambda i: (_mod_row(l, i, tm), 0, 0)),
            pl.BlockSpec((None, 1, D_MODEL), layer),
            pl.BlockSpec((None, 1, D_MODEL), layer),
            pl.BlockSpec((None, QA_W, D_MODEL), layer),
            pl.BlockSpec((None, OB_W, D_MODEL), layer),
            pl.BlockSpec((None, D_RNN, D_MODEL), layer),
            pl.BlockSpec((None, D_MODEL, 3 * D_MODEL), lambda i: (l, 0, OFF_GL // (3 * D_MODEL))),
            pl.BlockSpec((None, D_MODEL, D_MODEL), layer),
        ],
        out_specs=[pl.BlockSpec((tm, D_MODEL), row), pl.BlockSpec((tm, D_MODEL), row)],
        out_shape=[
            jax.ShapeDtypeStruct((N_TOK, D_MODEL), F32),
            jax.ShapeDtypeStruct((N_TOK, D_MODEL), BF16),
        ],
        compiler_params=_cparams(1),
        name="merge",
    )(*oa, *ob, *oc, *x, h, mods, gpost, gpre2, w_oa, w_ob, w_oc, w_gl, w_out)


def _ffn_kernel(hp_ref, hm_ref, hn_ref, x1_ref, mod_ref, gpost_ref, wu_ref, cw_ref, cb_ref, wd_ref, o_ref, hext_ref,
                *, t_len):
    tm = TM_FFN
    if t_len >= tm:
        tiles_per_seq = t_len // tm
        pos = pl.program_id(0) % tiles_per_seq
        hext_ref[0:HALO, :] = jnp.where(pos == 0, jnp.zeros_like(hp_ref[...]), hp_ref[...])
        hext_ref[HALO + tm:, :] = jnp.where(pos == tiles_per_seq - 1, jnp.zeros_like(hn_ref[...]), hn_ref[...])
        inner = []
    else:
        hext_ref[0:HALO, :] = jnp.zeros((HALO, D_MODEL), BF16)
        hext_ref[HALO + tm:, :] = jnp.zeros((HALO, D_MODEL), BF16)
        inner = list(range(t_len, tm, t_len))
    hext_ref[HALO:HALO + tm, :] = hm_ref[...]
    hext = hext_ref[...]
    n_ext = tm + 2 * HALO
    sub = lax.broadcasted_iota(jnp.int32, (SUBLANES, 1), 0)

    def patch(x, row):
        s0 = row // SUBLANES * SUBLANES
        slab = jnp.where(sub == row - s0, 0.0, x[s0:s0 + SUBLANES])
        return jnp.concatenate([x[:s0], slab, x[s0 + SUBLANES:]], axis=0)

    def conv(cols):
        up = _dot(hext, wu_ref[:, cols])
        prev = pltpu.roll(up, 1, 0)[HALO:HALO + tm]
        nxt = pltpu.roll(up, n_ext - 1, 0)[HALO:HALO + tm]
        for r in inner:
            prev = patch(prev, r)
            nxt = patch(nxt, r - 1)
        return cw_ref[0:1, cols] * prev + cw_ref[1:2, cols] * up[HALO:HALO + tm] + cw_ref[2:3, cols] * nxt + cb_ref[:, cols]

    acc = jnp.zeros((tm, D_MODEL), F32)
    for c0, c1 in FFN_CHUNKS:
        val = conv(slice(c0, c1))
        gat = conv(slice(D_FF + c0, D_FF + c1))
        acc = acc + _dot((jax.nn.gelu(gat) * val).astype(BF16), wd_ref[c0:c1, :])
    gt2 = mod_ref[:, 5 * D_MODEL:6 * D_MODEL]
    o_ref[...] = x1_ref[...] + gt2 * _rms(acc, gpost_ref[...])


def _ffn(l, h2, x1, mods, gpost, w_up, conv_w, conv_b, w_down, *, tok0, n_tok, t_len, name):
    tm = TM_FFN
    per = tm // HALO
    n_halo_blocks = N_TOK // HALO
    t0 = tok0 // tm
    row = lambda i: (t0 + i, 0)
    resident = lambda shape: pl.BlockSpec(shape, lambda i: (l, 0, 0), pipeline_mode=pl.Buffered(1))
    return pl.pallas_call(
        functools.partial(_ffn_kernel, t_len=t_len),
        grid=(n_tok // tm,),
        in_specs=[
            pl.BlockSpec((HALO, D_MODEL), lambda i: (jnp.maximum((t0 + i) * per - 1, 0), 0)),
            pl.BlockSpec((tm, D_MODEL), row),
            pl.BlockSpec((HALO, D_MODEL), lambda i: (jnp.minimum((t0 + i + 1) * per, n_halo_blocks - 1), 0)),
            pl.BlockSpec((tm, D_MODEL), row),
            pl.BlockSpec((None, 1, 6 * D_MODEL), lambda i: (_mod_row(l, t0 + i, tm), 0, 0)),
            pl.BlockSpec((None, 1, D_MODEL), lambda i: (l, 0, 0)),
            resident((None, D_MODEL, 2 * D_FF)),
            resident((None, 3, 2 * D_FF)),
            resident((None, 1, 2 * D_FF)),
            resident((None, D_FF, D_MODEL)),
        ],
        out_specs=pl.BlockSpec((tm, D_MODEL), lambda i: (i, 0)),
        out_shape=jax.ShapeDtypeStruct((n_tok, D_MODEL), F32),
        scratch_shapes=[pltpu.VMEM((tm + 2 * HALO, D_MODEL), BF16)],
        compiler_params=_cparams(1),
        name=name,
    )(h2, h2, h2, x1, mods, gpost, w_up, conv_w, conv_b, w_down)


def _rope_tables():
    t = np.arange(DEC_SEQ)
    row = (t // GRID_W).astype(np.float64)[:, None]
    col = (t % GRID_W).astype(np.float64)[:, None]

    def parts(dim):
        n = dim // 4
        inv = ROPE_THETA ** (-np.arange(n, dtype=np.float64) / n)
        ar, ac = row * inv, col * inv
        z = np.zeros_like(ar)
        cos = np.concatenate([np.cos(ar), np.cos(ar), np.cos(ac), np.cos(ac)], axis=-1)
        s_up = np.concatenate([-np.sin(ar), z, -np.sin(ac), z], axis=-1)
        s_dn = np.concatenate([z, np.sin(ar), z, np.sin(ac)], axis=-1)
        return cos, s_up, s_dn

    a = [np.tile(p, (1, LANES // HD_A)) for p in parts(HD_A)]
    pad = lambda p, fill: np.concatenate(
        [np.full((DEC_SEQ, KR_LANE), fill), p, np.full((DEC_SEQ, LANES - KR_LANE - ROPE_B), fill)], axis=-1)
    cb, sbu, sbd = parts(ROPE_B)
    lat = np.concatenate(a + [pad(cb, 1.0), pad(sbu, 0.0), pad(sbd, 0.0)], axis=-1)
    ident_blk = np.concatenate([np.ones((TM_IN, LANES)), np.zeros((TM_IN, 2 * LANES))], axis=-1)
    ident = np.concatenate([ident_blk, ident_blk], axis=-1)
    return jnp.asarray(np.concatenate([ident, lat], axis=0), F32)


def _pack_proj_kernel(w_ref, o_ref):
    rows = o_ref.shape[0]

    def section(src, n):
        return w_ref[src:src + n, :].T

    def zero(dst, n):
        o_ref[:, dst:dst + n] = jnp.zeros((rows, n), BF16)

    d_qb = NOPE_B + ROPE_B
    o_ref[:, OFF_QA:OFF_QB] = section(0, OFF_QB).astype(BF16)
    qb = section(OFF_QB, QB_W)
    for hd in range(H_B):
        o_ref[:, OFF_QB + hd * LANES:OFF_QB + hd * LANES + d_qb] = qb[:, hd * d_qb:(hd + 1) * d_qb].astype(BF16)
        zero(OFF_QB + hd * LANES + d_qb, LANES - d_qb)
    src_ckv = OFF_QB + QB_W
    o_ref[:, OFF_CKV:OFF_CKV + KV_RANK] = section(src_ckv, KV_RANK).astype(BF16)
    zero(OFF_KR, KR_LANE)
    kr = section(src_ckv + KV_RANK, LANES)[:, 0:ROPE_B]
    o_ref[:, OFF_KR + KR_LANE:OFF_KR + KR_LANE + ROPE_B] = kr.astype(BF16)
    zero(OFF_KR + KR_LANE + ROPE_B, OFF_XR - OFF_KR - KR_LANE - ROPE_B)
    src_xr = src_ckv + KV_RANK + ROPE_B
    o_ref[:, OFF_XR:OFF_XR + 2 * D_RNN] = section(src_xr, 2 * D_RNN).astype(BF16)
    zero(OFF_XR + 2 * D_RNN, OFF_GL - OFF_XR - 2 * D_RNN)
    o_ref[:, OFF_GL:OFF_GL + 3 * D_MODEL] = section(src_xr + 2 * D_RNN, 3 * D_MODEL).astype(BF16)


def _pack_w_proj(w_in):
    w_t = jnp.swapaxes(w_in, 1, 2)
    tr = 256
    return pl.pallas_call(
        _pack_proj_kernel,
        grid=(DEPTH, D_MODEL // tr),
        in_specs=[pl.BlockSpec((None, w_t.shape[1], tr), lambda l, r: (l, 0, r))],
        out_specs=pl.BlockSpec((None, tr, PROJ_W), lambda l, r: (l, r, 0)),
        out_shape=jax.ShapeDtypeStruct((DEPTH, D_MODEL, PROJ_W), BF16),
        compiler_params=_cparams(2),
        name="pack_proj",
    )(w_t)


def _pack_w_kv(w_uk, w_uv):
    uk = w_uk.reshape(DEPTH, KV_RANK, H_B, NOPE_B)
    k_part = jnp.pad(uk, ((0, 0), (0, 0), (0, 0), (0, LANES - NOPE_B))).reshape(DEPTH, KV_RANK, QB_PAD)
    return jnp.concatenate([k_part, w_uv], axis=-1).astype(BF16)


def _cache_dup_heads(cache, with_ones):
    x = jnp.transpose(cache, (1, 3, 0, 2, 4)).astype(BF16)
    parts = [x, x]
    if with_ones:
        parts.append(jnp.broadcast_to((jnp.arange(LANES) == 0).astype(BF16), x.shape[:-1] + (LANES,)))
    return jnp.concatenate(parts, axis=-1).reshape(DEPTH, KV_A, DEC_BATCH * PAST_LEN, -1)


def kernel(x_prompt, x_sample, c, cache_gqa_k, cache_gqa_v, cache_mla_ckv, cache_mla_krope, state_rglru_fwd, state_rglru_bwd, c_ctx, w_ada, b_ada, g_pre_mix, g_post_mix, g_pre_ffn, g_post_ffn, w_in, g_qa, g_ka, g_ckv, w_uk, w_uv, conv_rnn_w, conv_rnn_b, w_rg, b_rg, w_ig, b_ig, lam, w_oa, w_ob, w_oc, w_out, w_up, conv_ffn_w, conv_ffn_b, w_down):
    x = (x_prompt.reshape(N_CTX, D_MODEL), x_sample.reshape(N_LAT, D_MODEL))
    cvec = jnp.concatenate([c_ctx[None, :], c, jnp.zeros((N_MOD_ROWS - 1 - DEC_BATCH, D_MODEL), F32)], axis=0)
    mods = _modulation(cvec, w_ada, b_ada).reshape(DEPTH * N_MOD_ROWS, 1, 6 * D_MODEL)

    vec = lambda g: g.reshape(DEPTH, 1, -1)
    tab = _rope_tables()
    seg = np.arange(QA_W) // HD_A
    bd = jnp.asarray(np.where(seg[:, None] == seg[None, :], 1.0 / HD_A, 0.0), BF16)
    lane = np.arange(QB_PAD) % LANES
    place32_np = (lane[None, :] == KR_LANE + np.arange(ROPE_B)[:, None]).astype(np.float32)
    place32 = jnp.asarray(place32_np, BF16)
    place = jnp.asarray(np.pad(place32_np, ((KR_LANE, LANES - KR_LANE - ROPE_B), (0, 0))), BF16)
    w_att = w_xy = w_gl = _pack_w_proj(w_in)
    w_kv = _pack_w_kv(w_uk, w_uv)
    gqa_t = jnp.tile(g_qa, (1, H_A)).reshape(DEPTH, 1, QA_W)
    gka_t = jnp.tile(g_ka, (1, KV_A)).reshape(DEPTH, 1, KA_W)
    wg = (0.5 * jnp.concatenate([w_rg[:, 0], w_ig[:, 0], w_rg[:, 1], w_ig[:, 1]], axis=-1)).astype(BF16)
    bg = 0.5 * jnp.stack([b_rg[:, 0], b_ig[:, 0], b_rg[:, 1], b_ig[:, 1]], axis=1)
    h0_lat = jnp.stack([state_rglru_fwd, state_rglru_bwd], axis=0)
    h0_zero = jnp.zeros((BATCH, 2, D_RNN), F32)
    w_oa_b, w_ob_b, w_oc_b, w_out_b = (w.astype(BF16) for w in (w_oa, w_ob, w_oc, w_out))
    w_up_b, w_down_b = w_up.astype(BF16), w_down.astype(BF16)
    conv_ffn_b3 = vec(conv_ffn_b)

    kc_a = _cache_dup_heads(cache_gqa_k, False)
    vc_a = _cache_dup_heads(cache_gqa_v, True)
    kc_b, vc_b = _kvup_cache(cache_mla_ckv, cache_mla_krope, w_kv, place32)

    lat_blk0 = N_CTX // DEC_SEQ
    gqa_k = gqa_v = lambda j: j // 2
    mla_ke = lambda j: 2 * j
    mla_ko = lambda j: 2 * j + 1
    mla_v = lambda j: j

    def new_seg(k, v, rows, blk0, fe, fo, fv):
        at = lambda f: lambda b, j: (f(j), blk0 + b, 0)
        return (k, v, (None, rows, LANES), (None, rows, 2 * LANES), at(fe), at(fo), at(fv))

    new_k, new_v, new_ckv, new_kr, new_fwd, new_bwd = [], [], [], [], [], []
    for l in range(DEPTH):
        qa, qb, ka, va, kdup, vdup, ckv, kr, kb, vb, h = _inproj(
            l, *x, mods, vec(g_pre_mix), tab, w_att, bd, gqa_t, gka_t, vec(g_ckv), w_kv, place)

        def cache_seg(k, v, fe, fo, fv):
            at = lambda f: lambda b, j: (l, f(j), b, 0)
            return (k, v, (None, None, PAST_LEN, LANES), (None, None, PAST_LEN, 2 * LANES), at(fe), at(fo), at(fv))

        gqa_idx = (gqa_k, gqa_k, gqa_v)
        gqa = dict(qe_lane=lambda j: j, qo_lane=lambda j: j, mask_q=True)
        oa_ctx = _attention(qa, [new_seg(kdup, vdup, SEQ, 0, *gqa_idx)],
                            n_batch=BATCH, t_len=SEQ, tok0=0, pairs_per_step=N_PAIR, name="gqa_ctx", **gqa)
        oa_lat = _attention(qa, [cache_seg(kc_a, vc_a, *gqa_idx), new_seg(kdup, vdup, DEC_SEQ, lat_blk0, *gqa_idx)],
                            n_batch=DEC_BATCH, t_len=DEC_SEQ, tok0=N_CTX, pairs_per_step=2, name="gqa_lat", **gqa)
        mla_idx = (mla_ke, mla_ko, mla_v)
        mla = dict(qe_lane=mla_ke, qo_lane=mla_ko, mask_q=False)
        ob_ctx = _attention(qb, [new_seg(kb, vb, SEQ, 0, *mla_idx)],
                            n_batch=BATCH, t_len=SEQ, tok0=0, pairs_per_step=N_PAIR, name="mla_ctx", **mla)
        ob_lat = _attention(qb, [cache_seg(kc_b, vc_b, *mla_idx), new_seg(kb, vb, DEC_SEQ, lat_blk0, *mla_idx)],
                            n_batch=DEC_BATCH, t_len=DEC_SEQ, tok0=N_CTX, pairs_per_step=2, name="mla_lat", **mla)

        rnn_args = (w_xy, conv_rnn_w, vec(conv_rnn_b), wg, bg, lam)
        oc_ctx, st_ctx = _rglru(l, h, *rnn_args, h0_zero,
                                n_seq=BATCH, t_len=SEQ, tok0=0, n_sub=RNN_CTX_SUB, name="rglru_ctx")
        oc_lat, _ = _rglru(l, h, *rnn_args, jnp.moveaxis(h0_lat[:, :, l], 0, 1),
                           n_seq=DEC_BATCH, t_len=DEC_SEQ, tok0=N_CTX, n_sub=1, name="rglru_lat")

        x1, h2 = _merge(l, (oa_ctx, oa_lat), (ob_ctx, ob_lat), (oc_ctx, oc_lat), x, h, mods,
                        vec(g_post_mix), vec(g_pre_ffn), w_oa_b, w_ob_b, w_oc_b, w_gl, w_out_b)
        ffn_args = (l, h2, x1, mods, vec(g_post_ffn), w_up_b, conv_ffn_w, conv_ffn_b3, w_down_b)
        x = (_ffn(*ffn_args, tok0=0, n_tok=N_CTX, t_len=SEQ, name="ffn_ctx"),
             _ffn(*ffn_args, tok0=N_CTX, n_tok=N_LAT, t_len=DEC_SEQ, name="ffn_lat"))

        new_k.append(ka[:N_CTX].reshape(BATCH, SEQ, KV_A, HD_A))
        new_v.append(va[:N_CTX].reshape(BATCH, SEQ, KV_A, HD_A))
        new_ckv.append(ckv[:N_CTX].reshape(BATCH, SEQ, KV_RANK))
        new_kr.append(kr[:N_CTX].reshape(BATCH, SEQ, ROPE_B))
        new_fwd.append(st_ctx[:, 0])
        new_bwd.append(st_ctx[:, 1])

    stack = lambda xs: jnp.stack(xs, axis=1)
    return (x[0].reshape(BATCH, SEQ, D_MODEL), x[1].reshape(DEC_BATCH, DEC_SEQ, D_MODEL),
            stack(new_k), stack(new_v), stack(new_ckv), stack(new_kr), stack(new_fwd), stack(new_bwd))
```

```python
import functools
import math

import jax
import jax.numpy as jnp
import numpy as np
from jax import lax
from jax.experimental import pallas as pl
from jax.experimental.pallas import tpu as pltpu

F32 = jnp.float32
BF16 = jnp.bfloat16

D_MODEL = 1024
BATCH = 16
SEQ = 256
DEPTH = 2
DEC_BATCH = 4
DEC_SEQ = 2048
PAST_LEN = 512
GRID_W = 64
H_A = 8
KV_A = 2
HD_A = 64
H_B = 8
NOPE_B = 64
ROPE_B = 32
VD_B = 64
KV_RANK = 256
D_RNN = 1024
RNN_BLOCKS = 8
RNN_BS = D_RNN // RNN_BLOCKS
RG_C = 8.0
D_FF = 2816
ROPE_THETA = 10000.0
EPS = 1e-6
QA_W = H_A * HD_A
KA_W = KV_A * HD_A
QB_W = H_B * (NOPE_B + ROPE_B)
OB_W = H_B * VD_B

LANES = 128
SUBLANES = 8
N_CTX = BATCH * SEQ
N_LAT = DEC_BATCH * DEC_SEQ
N_TOK = N_CTX + N_LAT
N_MOD_ROWS = 8
QB_PAD = H_B * LANES
KR_LANE = NOPE_B
N_PAIR = 4
OFF_QA, OFF_KA, OFF_VA, OFF_QB, OFF_CKV, OFF_KR = 0, 512, 640, 768, 1792, 2048
ATT_W = OFF_KR + LANES
ATT_BLK = 2560
OFF_XR = ATT_BLK
OFF_GL = 6144
PROJ_W = OFF_GL + 3 * D_MODEL
KV_W = QB_PAD + OB_W
TAB_W = 6 * LANES
LOG2E = math.log2(math.e)

TM_IN = 256
TQ = 512
RNN_CB = 256
RNN_CTX_SUB = 4
TM_MERGE = 512
TM_FFN = 512
FFN_CHUNKS = ((0, 1536), (1536, D_FF))
HALO = 16
VMEM_LIMIT = 56 * 1024 * 1024


def _cparams(n_axes):
    return pltpu.CompilerParams(dimension_semantics=("arbitrary",) * n_axes, vmem_limit_bytes=VMEM_LIMIT)


def _dot(a, b):
    return jnp.dot(a, b, preferred_element_type=F32)


def _rms(x, g):
    return x * lax.rsqrt(jnp.mean(x * x, axis=-1, keepdims=True) + EPS) * g


def _mod_row(l, i, tm):
    n_ctx_tiles = N_CTX // tm
    return l * N_MOD_ROWS + jnp.where(i < n_ctx_tiles, 0, 1 + (i - n_ctx_tiles) // (DEC_SEQ // tm))


def _mod_kernel(c_ref, w_ref, b_ref, o_ref):
    c = c_ref[...]
    s = (c * jax.nn.sigmoid(c)).astype(BF16)
    o_ref[...] = _dot(s, w_ref[...].astype(BF16)) + b_ref[...]


def _modulation(cvec, w_ada, b_ada):
    tn = 1536
    return pl.pallas_call(
        _mod_kernel,
        grid=(DEPTH, 6 * D_MODEL // tn),
        in_specs=[
            pl.BlockSpec((N_MOD_ROWS, D_MODEL), lambda l, n: (0, 0)),
            pl.BlockSpec((None, D_MODEL, tn), lambda l, n: (l, 0, n)),
            pl.BlockSpec((None, 1, tn), lambda l, n: (l, 0, n)),
        ],
        out_specs=pl.BlockSpec((None, N_MOD_ROWS, tn), lambda l, n: (l, 0, n)),
        out_shape=jax.ShapeDtypeStruct((DEPTH, N_MOD_ROWS, 6 * D_MODEL), F32),
        compiler_params=_cparams(2),
        name="modulation",
    )(cvec, w_ada, b_ada.reshape(DEPTH, 1, 6 * D_MODEL))


def _seg_mean(x2, bd):
    hi = x2.astype(BF16)
    lo = (x2 - hi.astype(F32)).astype(BF16)
    return _dot(hi, bd) + _dot(lo, bd)


def _rope(x, cos, sin_up, sin_dn, shift):
    w = x.shape[-1]
    return x * cos + pltpu.roll(x, w - shift, 1) * sin_up + pltpu.roll(x, shift, 1) * sin_dn


def _dup_heads(x):
    lo = lax.broadcasted_iota(jnp.int32, x.shape, 1) < HD_A
    sw = pltpu.roll(x, HD_A, 1)
    return jnp.where(lo, x, sw).astype(BF16), jnp.where(lo, sw, x).astype(BF16)


def _ones_column(rows):
    return jnp.where(lax.broadcasted_iota(jnp.int32, (rows, LANES), 1) == 0, 1.0, 0.0).astype(BF16)


def _store_mla_kv(y, kr_all, ones, k_ref, v_ref):
    for hd in range(H_B):
        sl = slice(hd * LANES, (hd + 1) * LANES)
        k_ref[hd] = (y[:, sl] + kr_all[:, sl]).astype(BF16)
    for j in range(N_PAIR):
        v_ref[j, :, 0:LANES] = y[:, QB_PAD + j * LANES:QB_PAD + (j + 1) * LANES].astype(BF16)
        v_ref[j, :, LANES:2 * LANES] = ones


def _inproj_kernel(xc_ref, xl_ref, mod_ref, gpre_ref, tab_ref, w_ref, bd_ref, gqa_ref, gka_ref, gckv_ref, wkv_ref, place_ref,
                   qa_ref, qb_ref, ka_ref, va_ref, kdup_ref, vdup_ref, ckv_ref, kr_ref, kb_ref, vb_ref, h_ref):
    x = jnp.where(pl.program_id(0) < N_CTX // TM_IN, xc_ref[...], xl_ref[...])
    sh1 = mod_ref[:, 0:D_MODEL]
    sc1 = mod_ref[:, D_MODEL:2 * D_MODEL]
    h = (_rms(x, gpre_ref[...]) * (1.0 + sc1) + sh1).astype(BF16)
    h_ref[...] = h
    y = _dot(h, w_ref[:, 0:ATT_W])

    cos_a, sa_up, sa_dn = tab_ref[:, 0:128], tab_ref[:, 128:256], tab_ref[:, 256:384]
    cos_b, sb_up, sb_dn = tab_ref[:, 384:512], tab_ref[:, 512:640], tab_ref[:, 640:768]

    q = y[:, OFF_QA:OFF_QA + QA_W]
    q = q * lax.rsqrt(_seg_mean(q * q, bd_ref[...]) + EPS) * gqa_ref[...]
    rep = QA_W // LANES
    q = _rope(q, jnp.tile(cos_a, (1, rep)), jnp.tile(sa_up, (1, rep)), jnp.tile(sa_dn, (1, rep)), HD_A // 4)
    qa_ref[...] = (q * (HD_A ** -0.5 * LOG2E)).astype(BF16)

    k = y[:, OFF_KA:OFF_KA + KA_W]
    k = k * lax.rsqrt(_seg_mean(k * k, bd_ref[0:KA_W, 0:KA_W]) + EPS) * gka_ref[...]
    k = _rope(k, cos_a, sa_up, sa_dn, HD_A // 4)
    ka_ref[...] = k
    v = y[:, OFF_VA:OFF_VA + KA_W]
    va_ref[...] = v
    ones = _ones_column(k.shape[0])
    for n, (kd, vd) in enumerate(zip(_dup_heads(k), _dup_heads(v))):
        kdup_ref[n] = kd
        vdup_ref[n, :, 0:LANES] = vd
        vdup_ref[n, :, LANES:2 * LANES] = ones

    qb = y[:, OFF_QB:OFF_QB + QB_PAD]
    qb = _rope(qb, jnp.tile(cos_b, (1, H_B)), jnp.tile(sb_up, (1, H_B)), jnp.tile(sb_dn, (1, H_B)), ROPE_B // 4)
    qb_ref[...] = (qb * ((NOPE_B + ROPE_B) ** -0.5 * LOG2E)).astype(BF16)

    ckv = _rms(y[:, OFF_CKV:OFF_CKV + KV_RANK], gckv_ref[...])
    ckv_ref[...] = ckv
    kr = _rope(y[:, OFF_KR:OFF_KR + LANES], cos_b, sb_up, sb_dn, ROPE_B // 4)
    kr_ref[...] = kr[:, KR_LANE:KR_LANE + ROPE_B]

    y2 = _dot(ckv.astype(BF16), wkv_ref[...])
    kr_all = _dot(kr.astype(BF16), place_ref[...])
    _store_mla_kv(y2, kr_all, ones, kb_ref, vb_ref)


def _inproj(l, x_ctx, x_lat, mods, gpre, tab, w_att, bd, gqa_t, gka_t, gckv, w_kv, place):
    tm = TM_IN
    n_ctx_tiles = N_CTX // tm
    lat_tiles = DEC_SEQ // tm

    def tab_idx(i):
        return (jnp.where(i < n_ctx_tiles, 0, 1 + (i - n_ctx_tiles) % lat_tiles), 0)

    row = lambda i: (i, 0)
    row3 = lambda i: (0, i, 0)
    const = lambda i: (0, 0)
    layer = lambda i: (l, 0, 0)
    return pl.pallas_call(
        _inproj_kernel,
        grid=(N_TOK // tm,),
        in_specs=[
            pl.BlockSpec((tm, D_MODEL), lambda i: (jnp.minimum(i, n_ctx_tiles - 1), 0)),
            pl.BlockSpec((tm, D_MODEL), lambda i: (jnp.maximum(i - n_ctx_tiles, 0), 0)),
            pl.BlockSpec((None, 1, 6 * D_MODEL), lambda i: (_mod_row(l, i, tm), 0, 0)),
            pl.BlockSpec((None, 1, D_MODEL), layer),
            pl.BlockSpec((tm, TAB_W), tab_idx),
            pl.BlockSpec((None, D_MODEL, ATT_BLK), layer),
            pl.BlockSpec((QA_W, QA_W), const),
            pl.BlockSpec((None, 1, QA_W), layer),
            pl.BlockSpec((None, 1, KA_W), layer),
            pl.BlockSpec((None, 1, KV_RANK), layer),
            pl.BlockSpec((None, KV_RANK, KV_W), layer),
            pl.BlockSpec((LANES, QB_PAD), const),
        ],
        out_specs=[
            pl.BlockSpec((tm, QA_W), row),
            pl.BlockSpec((tm, QB_PAD), row),
            pl.BlockSpec((tm, KA_W), row),
            pl.BlockSpec((tm, KA_W), row),
            pl.BlockSpec((KV_A, tm, LANES), row3),
            pl.BlockSpec((KV_A, tm, 2 * LANES), row3),
            pl.BlockSpec((tm, KV_RANK), row),
            pl.BlockSpec((tm, ROPE_B), row),
            pl.BlockSpec((H_B, tm, LANES), row3),
            pl.BlockSpec((N_PAIR, tm, 2 * LANES), row3),
            pl.BlockSpec((tm, D_MODEL), row),
        ],
        out_shape=[
            jax.ShapeDtypeStruct((N_TOK, QA_W), BF16),
            jax.ShapeDtypeStruct((N_TOK, QB_PAD), BF16),
            jax.ShapeDtypeStruct((N_TOK, KA_W), F32),
            jax.ShapeDtypeStruct((N_TOK, KA_W), F32),
            jax.ShapeDtypeStruct((KV_A, N_TOK, LANES), BF16),
            jax.ShapeDtypeStruct((KV_A, N_TOK, 2 * LANES), BF16),
            jax.ShapeDtypeStruct((N_TOK, KV_RANK), F32),
            jax.ShapeDtypeStruct((N_TOK, ROPE_B), F32),
            jax.ShapeDtypeStruct((H_B, N_TOK, LANES), BF16),
            jax.ShapeDtypeStruct((N_PAIR, N_TOK, 2 * LANES), BF16),
            jax.ShapeDtypeStruct((N_TOK, D_MODEL), BF16),
        ],
        compiler_params=_cparams(1),
        name="inproj",
    )(x_ctx, x_lat, mods, gpre, tab, w_att, bd, gqa_t, gka_t, gckv, w_kv, place)


def _kvup_cache_kernel(c_ref, r_ref, w_ref, place_ref, k_ref, v_ref):
    y = _dot(c_ref[...].astype(BF16), w_ref[...])
    kr_all = _dot(r_ref[...].astype(BF16), place_ref[...])
    _store_mla_kv(y, kr_all, _ones_column(y.shape[0]), k_ref, v_ref)


def _kvup_cache(cache_ckv, cache_kr, w_kv, place32):
    rows = DEC_BATCH * PAST_LEN
    idx = lambda l, b: (l, 0, b, 0)
    return pl.pallas_call(
        _kvup_cache_kernel,
        grid=(DEPTH, DEC_BATCH),
        in_specs=[
            pl.BlockSpec((None, None, PAST_LEN, KV_RANK), lambda l, b: (b, l, 0, 0)),
            pl.BlockSpec((None, None, PAST_LEN, ROPE_B), lambda l, b: (b, l, 0, 0)),
            pl.BlockSpec((None, KV_RANK, KV_W), lambda l, b: (l, 0, 0)),
            pl.BlockSpec((ROPE_B, QB_PAD), lambda l, b: (0, 0)),
        ],
        out_specs=[pl.BlockSpec((None, H_B, PAST_LEN, LANES), idx), pl.BlockSpec((None, N_PAIR, PAST_LEN, 2 * LANES), idx)],
        out_shape=[jax.ShapeDtypeStruct((DEPTH, H_B, rows, LANES), BF16),
                   jax.ShapeDtypeStruct((DEPTH, N_PAIR, rows, 2 * LANES), BF16)],
        compiler_params=_cparams(2),
        name="kvup_cache",
    )(cache_ckv, cache_kr, w_kv, place32)


def _attn_kernel(*refs, n_seg, n_pair, mask_q):
    per_pair = 2 + 3 * n_seg
    o_ref = refs[-1]
    lo = lax.broadcasted_iota(jnp.int32, (o_ref.shape[0], LANES), 1) < HD_A

    def scores(q, ks):
        return [lax.dot_general(q, k[...], (((1,), (1,)), ((), ())), preferred_element_type=F32) for k in ks]

    def attend(ss, vs):
        m = functools.reduce(jnp.maximum, [jnp.max(s, axis=-1, keepdims=True) for s in ss])
        full = functools.reduce(jnp.add, [_dot(jnp.exp2(s - m).astype(BF16), v[...]) for s, v in zip(ss, vs)])
        return full[:, 0:LANES] / full[:, LANES:LANES + 1]

    for p in range(n_pair):
        r = refs[p * per_pair:(p + 1) * per_pair]
        k_refs, v_refs = r[2:2 + 2 * n_seg], r[2 + 2 * n_seg:]
        qe, qo = r[0][...], r[1][...]
        if mask_q:
            qe = jnp.where(lo, qe, jnp.zeros_like(qe))
            qo = jnp.where(lo, jnp.zeros_like(qo), qo)
        ss_e = scores(qe, k_refs[0::2])
        ss_o = scores(qo, k_refs[1::2])
        o_ref[:, p * LANES:(p + 1) * LANES] = jnp.where(lo, attend(ss_e, v_refs), attend(ss_o, v_refs)).astype(o_ref.dtype)


def _attention(q, segs, *, n_batch, t_len, tok0, qe_lane, qo_lane, mask_q, pairs_per_step, name):
    tq = min(TQ, t_len)
    nq = t_len // tq
    q_blk0 = tok0 // tq
    npp = pairs_per_step

    def q_map(lane_fn, p):
        return lambda b, g, i: (q_blk0 + b * nq + i, lane_fn(g * npp + p))

    def kv_map(f, p):
        return lambda b, g, i: f(b, g * npp + p)

    in_specs, args = [], []
    for p in range(npp):
        in_specs += [pl.BlockSpec((tq, LANES), q_map(qe_lane, p)), pl.BlockSpec((tq, LANES), q_map(qo_lane, p))]
        args += [q, q]
        for k, _, k_blk, _, ke_idx, ko_idx, _ in segs:
            in_specs += [pl.BlockSpec(k_blk, kv_map(ke_idx, p)), pl.BlockSpec(k_blk, kv_map(ko_idx, p))]
            args += [k, k]
        for _, v, _, v_blk, _, _, v_idx in segs:
            in_specs.append(pl.BlockSpec(v_blk, kv_map(v_idx, p)))
            args.append(v)
    return pl.pallas_call(
        functools.partial(_attn_kernel, n_seg=len(segs), n_pair=npp, mask_q=mask_q),
        grid=(n_batch, N_PAIR // npp, nq),
        in_specs=in_specs,
        out_specs=pl.BlockSpec((tq, npp * LANES), lambda b, g, i: (b * nq + i, g)),
        out_shape=jax.ShapeDtypeStruct((n_batch * t_len, N_PAIR * LANES), BF16),
        compiler_params=_cparams(3),
        name=name,
    )(*args)


def _rglru_kernel(h_ref, wx_ref, wy_ref, cw_ref, cb_ref, wg_ref, bg_ref, lam_ref, h0_ref,
                  oc_ref, st_ref, af_ref, bf_ref, ab_ref, bb_ref, *, t_len, n_sub):
    n_blk = oc_ref.shape[1] // RNN_BS
    sub = lax.broadcasted_iota(jnp.int32, (SUBLANES, 1), 0)
    lam = lam_ref[...]
    half_c = (0.5 * RG_C) * (jnp.minimum(lam, 0.0) - jnp.log1p(jnp.exp(-jnp.abs(lam))))

    for s in range(n_sub):
        xr = _dot(h_ref[s * t_len:(s + 1) * t_len, :], wx_ref[...])

        def shifted(shift, keep_first, keep_last):
            x = pltpu.roll(xr, shift % t_len, 0)
            head = jnp.where(keep_first, x[:SUBLANES], 0.0)
            tail = jnp.where(keep_last, x[t_len - SUBLANES:], 0.0)
            return jnp.concatenate([head, x[SUBLANES:t_len - SUBLANES], tail], axis=0)

        x_m2 = shifted(2, sub >= 2, True)
        x_m1 = shifted(1, sub >= 1, True)
        x_p1 = shifted(-1, True, sub < SUBLANES - 1)
        u = cw_ref[0:1, :] * x_m2 + cw_ref[1:2, :] * x_m1 + cw_ref[2:3, :] * xr + cw_ref[3:4, :] * x_p1 + cb_ref[...]
        for j in range(n_blk):
            sl = slice(j * RNN_BS, (j + 1) * RNN_BS)
            uj = u[:, sl]
            half_u = 0.5 * uj
            g = _dot(uj.astype(BF16), wg_ref[j])
            for d, (a_ref, b_ref) in enumerate(((af_ref, bf_ref), (ab_ref, bb_ref))):
                tr = jnp.tanh(g[:, (2 * d) * RNN_BS:(2 * d + 1) * RNN_BS] + bg_ref[2 * d:2 * d + 1, sl])
                ti = jnp.tanh(g[:, (2 * d + 1) * RNN_BS:(2 * d + 2) * RNN_BS] + bg_ref[2 * d + 1:2 * d + 2, sl])
                c = half_c[d:d + 1, sl]
                a = jnp.exp(c * tr + c)
                a_ref[s * n_blk + j] = a
                om = 1.0 - a * a
                root = jnp.where(om > 0.0, om * lax.rsqrt(om), 0.0)
                b_ref[s * n_blk + j] = root * (half_u * ti + half_u)

    row = lax.broadcasted_iota(jnp.int32, (SUBLANES, RNN_BS), 0)
    shifts = [k for k in (1, 2, 4) if k < SUBLANES]

    def tile_scan(a, b, entry, reverse):
        for k in shifts:
            if reverse:
                live = row < SUBLANES - k
                a_sh, b_sh = pltpu.roll(a, SUBLANES - k, 0), pltpu.roll(b, SUBLANES - k, 0)
            else:
                live = row >= k
                a_sh, b_sh = pltpu.roll(a, k, 0), pltpu.roll(b, k, 0)
            b = b + a * jnp.where(live, b_sh, 0.0)
            a = a * jnp.where(live, a_sh, 1.0)
        hs = b + a * entry
        last = hs[0:1, :] if reverse else hs[SUBLANES - 1:SUBLANES, :]
        return hs, jnp.broadcast_to(last, hs.shape)

    n_tile = t_len // SUBLANES
    n_chain = n_sub * n_blk

    def step(t, carry):
        fwd = pl.ds(pl.multiple_of(t * SUBLANES, SUBLANES), SUBLANES)
        bwd = pl.ds(pl.multiple_of((n_tile - 1 - t) * SUBLANES, SUBLANES), SUBLANES)
        out = []
        for q in range(n_chain):
            hs, ef = tile_scan(af_ref[q, fwd, :], bf_ref[q, fwd, :], carry[2 * q], False)
            bf_ref[q, fwd, :] = hs
            hs, eb = tile_scan(ab_ref[q, bwd, :], bb_ref[q, bwd, :], carry[2 * q + 1], True)
            bb_ref[q, bwd, :] = hs
            out += [ef, eb]
        return tuple(out)

    init = []
    for s in range(n_sub):
        for j in range(n_blk):
            sl = slice(j * RNN_BS, (j + 1) * RNN_BS)
            init += [jnp.broadcast_to(h0_ref[s, 0:1, sl], (SUBLANES, RNN_BS)),
                     jnp.broadcast_to(h0_ref[s, 1:2, sl], (SUBLANES, RNN_BS))]
    carry = lax.fori_loop(0, n_tile, step, tuple(init), unroll=2 if n_chain <= 2 else 1)

    for s in range(n_sub):
        rows = slice(s * t_len, (s + 1) * t_len)
        yr = _dot(h_ref[rows, :], wy_ref[...])
        for j in range(n_blk):
            sl = slice(j * RNN_BS, (j + 1) * RNN_BS)
            q = s * n_blk + j
            st_ref[s, 0:1, sl] = carry[2 * q][0:1, :]
            st_ref[s, 1:2, sl] = carry[2 * q + 1][0:1, :]
            oc_ref[rows, sl] = (jax.nn.gelu(yr[:, sl]) * (bf_ref[q] + bb_ref[q])).astype(BF16)


def _rglru(l, h_all, w_xy, conv_w, conv_b, wg, bg, lam, h0, *, n_seq, t_len, tok0, n_sub, cb, name):
    assert OFF_XR % cb == 0 and D_RNN % cb == 0 and cb % RNN_BS == 0
    nb = D_RNN // cb
    rows = n_sub * t_len
    blk0 = tok0 // rows
    chan3 = lambda b, n: (l, 0, n)
    state = pl.BlockSpec((n_sub, 2, cb), lambda b, n: (b, 0, n))
    return pl.pallas_call(
        functools.partial(_rglru_kernel, t_len=t_len, n_sub=n_sub),
        grid=(n_seq // n_sub, nb),
        in_specs=[
            pl.BlockSpec((rows, D_MODEL), lambda b, n: (blk0 + b, 0)),
            pl.BlockSpec((None, D_MODEL, cb), lambda b, n: (l, 0, OFF_XR // cb + n)),
            pl.BlockSpec((None, D_MODEL, cb), lambda b, n: (l, 0, OFF_XR // cb + nb + n)),
            pl.BlockSpec((None, 4, cb), chan3),
            pl.BlockSpec((None, 1, cb), chan3),
            pl.BlockSpec((None, cb // RNN_BS, RNN_BS, 4 * RNN_BS), lambda b, n: (l, n, 0, 0)),
            pl.BlockSpec((None, 4, cb), chan3),
            pl.BlockSpec((None, 2, cb), chan3),
            state,
        ],
        out_specs=[pl.BlockSpec((rows, cb), lambda b, n: (b, n)), state],
        out_shape=[
            jax.ShapeDtypeStruct((n_seq * t_len, D_RNN), BF16),
            jax.ShapeDtypeStruct((n_seq, 2, D_RNN), F32),
        ],
        scratch_shapes=[pltpu.VMEM((n_sub * cb // RNN_BS, t_len, RNN_BS), F32)] * 4,
        compiler_params=_cparams(2),
        name=name,
    )(h_all, w_xy, w_xy, conv_w, conv_b, wg, bg, lam, h0)


def _merge_kernel(oac_ref, oal_ref, obc_ref, obl_ref, occ_ref, ocl_ref, xc_ref, xl_ref, h_ref, mod_ref, gpost_ref, gpre2_ref,
                  woa_ref, wob_ref, woc_ref, wgl_ref, wout_ref, x1_ref, h2_ref):
    h = h_ref[...]
    is_ctx = pl.program_id(0) < N_CTX // TM_MERGE

    def gate(k):
        return jax.nn.sigmoid(_dot(h, wgl_ref[:, k * D_MODEL:(k + 1) * D_MODEL]))

    def branch(c_ref, l_ref, w_ref):
        return _dot(jnp.where(is_ctx, c_ref[...], l_ref[...]), w_ref[...])

    merged = gate(0) * branch(oac_ref, oal_ref, woa_ref)
    merged = merged + gate(1) * branch(obc_ref, obl_ref, wob_ref)
    merged = merged + gate(2) * branch(occ_ref, ocl_ref, woc_ref)
    out = _dot(merged.astype(BF16), wout_ref[...])
    gt1 = mod_ref[:, 2 * D_MODEL:3 * D_MODEL]
    sh2 = mod_ref[:, 3 * D_MODEL:4 * D_MODEL]
    sc2 = mod_ref[:, 4 * D_MODEL:5 * D_MODEL]
    x1 = jnp.where(is_ctx, xc_ref[...], xl_ref[...]) + gt1 * _rms(out, gpost_ref[...])
    x1_ref[...] = x1
    h2_ref[...] = (_rms(x1, gpre2_ref[...]) * (1.0 + sc2) + sh2).astype(BF16)


def _merge(l, oa, ob, oc, x, h, mods, gpost, gpre2, w_oa, w_ob, w_oc, w_gl, w_out):
    assert OFF_GL % (3 * D_MODEL) == 0
    tm = TM_MERGE
    nct = N_CTX // tm
    row = lambda i: (i, 0)
    ctx = lambda i: (jnp.minimum(i, nct - 1), 0)
    lat = lambda i: (jnp.maximum(i - nct, 0), 0)
    layer = lambda i: (l, 0, 0)

    def pair(width):
        return [pl.BlockSpec((tm, width), ctx), pl.BlockSpec((tm, width), lat)]

    return pl.pallas_call(
        _merge_kernel,
        grid=(N_TOK // tm,),
        in_specs=pair(QA_W) + pair(OB_W) + pair(D_RNN) + pair(D_MODEL) + [
            pl.BlockSpec((tm, D_MODEL), row),
            pl.BlockSpec((None, 1, 6 * D_MODEL), lambda i: (_mod_row(l, i, tm), 0, 0)),
            pl.BlockSpec((None, 1, D_MODEL), layer),
            pl.BlockSpec((None, 1, D_MODEL), layer),
            pl.BlockSpec((None, QA_W, D_MODEL), layer),
            pl.BlockSpec((None, OB_W, D_MODEL), layer),
            pl.BlockSpec((None, D_RNN, D_MODEL), layer),
            pl.BlockSpec((None, D_MODEL, 3 * D_MODEL), lambda i: (l, 0, OFF_GL // (3 * D_MODEL))),
            pl.BlockSpec((None, D_MODEL, D_MODEL), layer),
        ],
        out_specs=[pl.BlockSpec((tm, D_MODEL), row), pl.BlockSpec((tm, D_MODEL), row)],
        out_shape=[
            jax.ShapeDtypeStruct((N_TOK, D_MODEL), F32),
            jax.ShapeDtypeStruct((N_TOK, D_MODEL), BF16),
        ],
        compiler_params=_cparams(1),
        name="merge",
    )(*oa, *ob, *oc, *x, h, mods, gpost, gpre2, w_oa, w_ob, w_oc, w_gl, w_out)


def _ffn_kernel(hp_ref, hm_ref, hn_ref, x1_ref, mod_ref, gpost_ref, wu_ref, cw_ref, cb_ref, wd_ref, o_ref, hext_ref,
                *, t_len):
    tm = TM_FFN
    if t_len >= tm:
        tiles_per_seq = t_len // tm
        pos = pl.program_id(0) % tiles_per_seq
        hext_ref[0:HALO, :] = jnp.where(pos == 0, jnp.zeros_like(hp_ref[...]), hp_ref[...])
        hext_ref[HALO + tm:, :] = jnp.where(pos == tiles_per_seq - 1, jnp.zeros_like(hn_ref[...]), hn_ref[...])
        inner = []
    else:
        hext_ref[0:HALO, :] = jnp.zeros((HALO, D_MODEL), BF16)
        hext_ref[HALO + tm:, :] = jnp.zeros((HALO, D_MODEL), BF16)
        inner = list(range(t_len, tm, t_len))
    hext_ref[HALO:HALO + tm, :] = hm_ref[...]
    hext = hext_ref[...]
    n_ext = tm + 2 * HALO
    sub = lax.broadcasted_iota(jnp.int32, (SUBLANES, 1), 0)

    def patch(x, row):
        s0 = row // SUBLANES * SUBLANES
        slab = jnp.where(sub == row - s0, 0.0, x[s0:s0 + SUBLANES])
        return jnp.concatenate([x[:s0], slab, x[s0 + SUBLANES:]], axis=0)

    def conv(cols):
        up = _dot(hext, wu_ref[:, cols])
        prev = pltpu.roll(up, 1, 0)[HALO:HALO + tm]
        nxt = pltpu.roll(up, n_ext - 1, 0)[HALO:HALO + tm]
        for r in inner:
            prev = patch(prev, r)
            nxt = patch(nxt, r - 1)
        return cw_ref[0:1, cols] * prev + cw_ref[1:2, cols] * up[HALO:HALO + tm] + cw_ref[2:3, cols] * nxt + cb_ref[:, cols]

    acc = jnp.zeros((tm, D_MODEL), F32)
    for c0, c1 in FFN_CHUNKS:
        val = conv(slice(c0, c1))
        gat = conv(slice(D_FF + c0, D_FF + c1))
        acc = acc + _dot((jax.nn.gelu(gat) * val).astype(BF16), wd_ref[c0:c1, :])
    gt2 = mod_ref[:, 5 * D_MODEL:6 * D_MODEL]
    o_ref[...] = x1_ref[...] + gt2 * _rms(acc, gpost_ref[...])


def _ffn(l, h2, x1, mods, gpost, w_up, conv_w, conv_b, w_down, *, tok0, n_tok, t_len, name):
    tm = TM_FFN
    per = tm // HALO
    n_halo_blocks = N_TOK // HALO
    t0 = tok0 // tm
    row = lambda i: (t0 + i, 0)
    resident = lambda shape: pl.BlockSpec(shape, lambda i: (l, 0, 0), pipeline_mode=pl.Buffered(1))
    return pl.pallas_call(
        functools.partial(_ffn_kernel, t_len=t_len),
        grid=(n_tok // tm,),
        in_specs=[
            pl.BlockSpec((HALO, D_MODEL), lambda i: (jnp.maximum((t0 + i) * per - 1, 0), 0)),
            pl.BlockSpec((tm, D_MODEL), row),
            pl.BlockSpec((HALO, D_MODEL), lambda i: (jnp.minimum((t0 + i + 1) * per, n_halo_blocks - 1), 0)),
            pl.BlockSpec((tm, D_MODEL), row),
            pl.BlockSpec((None, 1, 6 * D_MODEL), lambda i: (_mod_row(l, t0 + i, tm), 0, 0)),
            pl.BlockSpec((None, 1, D_MODEL), lambda i: (l, 0, 0)),
            resident((None, D_MODEL, 2 * D_FF)),
            resident((None, 3, 2 * D_FF)),
            resident((None, 1, 2 * D_FF)),
            resident((None, D_FF, D_MODEL)),
        ],
        out_specs=pl.BlockSpec((tm, D_MODEL), lambda i: (i, 0)),
        out_shape=jax.ShapeDtypeStruct((n_tok, D_MODEL), F32),
        scratch_shapes=[pltpu.VMEM((tm + 2 * HALO, D_MODEL), BF16)],
        compiler_params=_cparams(1),
        name=name,
    )(h2, h2, h2, x1, mods, gpost, w_up, conv_w, conv_b, w_down)


def _rope_tables():
    t = np.arange(DEC_SEQ)
    row = (t // GRID_W).astype(np.float64)[:, None]
    col = (t % GRID_W).astype(np.float64)[:, None]

    def parts(dim):
        n = dim // 4
        inv = ROPE_THETA ** (-np.arange(n, dtype=np.float64) / n)
        ar, ac = row * inv, col * inv
        z = np.zeros_like(ar)
        cos = np.concatenate([np.cos(ar), np.cos(ar), np.cos(ac), np.cos(ac)], axis=-1)
        s_up = np.concatenate([-np.sin(ar), z, -np.sin(ac), z], axis=-1)
        s_dn = np.concatenate([z, np.sin(ar), z, np.sin(ac)], axis=-1)
        return cos, s_up, s_dn

    a = [np.tile(p, (1, LANES // HD_A)) for p in parts(HD_A)]
    pad = lambda p, fill: np.concatenate(
        [np.full((DEC_SEQ, KR_LANE), fill), p, np.full((DEC_SEQ, LANES - KR_LANE - ROPE_B), fill)], axis=-1)
    cb, sbu, sbd = parts(ROPE_B)
    lat = np.concatenate(a + [pad(cb, 1.0), pad(sbu, 0.0), pad(sbd, 0.0)], axis=-1)
    ident_blk = np.concatenate([np.ones((TM_IN, LANES)), np.zeros((TM_IN, 2 * LANES))], axis=-1)
    ident = np.concatenate([ident_blk, ident_blk], axis=-1)
    return jnp.asarray(np.concatenate([ident, lat], axis=0), F32)


def _pack_proj_kernel(w_ref, o_ref):
    rows = o_ref.shape[0]

    def section(src, n):
        return w_ref[src:src + n, :].T

    def zero(dst, n):
        o_ref[:, dst:dst + n] = jnp.zeros((rows, n), BF16)

    d_qb = NOPE_B + ROPE_B
    o_ref[:, OFF_QA:OFF_QB] = section(0, OFF_QB).astype(BF16)
    qb = section(OFF_QB, QB_W)
    for hd in range(H_B):
        o_ref[:, OFF_QB + hd * LANES:OFF_QB + hd * LANES + d_qb] = qb[:, hd * d_qb:(hd + 1) * d_qb].astype(BF16)
        zero(OFF_QB + hd * LANES + d_qb, LANES - d_qb)
    src_ckv = OFF_QB + QB_W
    o_ref[:, OFF_CKV:OFF_CKV + KV_RANK] = section(src_ckv, KV_RANK).astype(BF16)
    zero(OFF_KR, KR_LANE)
    kr = section(src_ckv + KV_RANK, LANES)[:, 0:ROPE_B]
    o_ref[:, OFF_KR + KR_LANE:OFF_KR + KR_LANE + ROPE_B] = kr.astype(BF16)
    zero(OFF_KR + KR_LANE + ROPE_B, OFF_XR - OFF_KR - KR_LANE - ROPE_B)
    src_xr = src_ckv + KV_RANK + ROPE_B
    o_ref[:, OFF_XR:OFF_XR + 2 * D_RNN] = section(src_xr, 2 * D_RNN).astype(BF16)
    zero(OFF_XR + 2 * D_RNN, OFF_GL - OFF_XR - 2 * D_RNN)
    o_ref[:, OFF_GL:OFF_GL + 3 * D_MODEL] = section(src_xr + 2 * D_RNN, 3 * D_MODEL).astype(BF16)


def _pack_w_proj(w_in):
    w_t = jnp.swapaxes(w_in, 1, 2)
    tr = 256
    return pl.pallas_call(
        _pack_proj_kernel,
        grid=(DEPTH, D_MODEL // tr),
        in_specs=[pl.BlockSpec((None, w_t.shape[1], tr), lambda l, r: (l, 0, r))],
        out_specs=pl.BlockSpec((None, tr, PROJ_W), lambda l, r: (l, r, 0)),
        out_shape=jax.ShapeDtypeStruct((DEPTH, D_MODEL, PROJ_W), BF16),
        compiler_params=_cparams(2),
        name="pack_proj",
    )(w_t)


def _pack_w_kv(w_uk, w_uv):
    uk = w_uk.reshape(DEPTH, KV_RANK, H_B, NOPE_B)
    k_part = jnp.pad(uk, ((0, 0), (0, 0), (0, 0), (0, LANES - NOPE_B))).reshape(DEPTH, KV_RANK, QB_PAD)
    return jnp.concatenate([k_part, w_uv], axis=-1).astype(BF16)


def _cache_dup_heads(cache, with_ones):
    x = jnp.transpose(cache, (1, 3, 0, 2, 4)).astype(BF16)
    parts = [x, x]
    if with_ones:
        parts.append(jnp.broadcast_to((jnp.arange(LANES) == 0).astype(BF16), x.shape[:-1] + (LANES,)))
    return jnp.concatenate(parts, axis=-1).reshape(DEPTH, KV_A, DEC_BATCH * PAST_LEN, -1)


def kernel(x_prompt, x_sample, c, cache_gqa_k, cache_gqa_v, cache_mla_ckv, cache_mla_krope, state_rglru_fwd, state_rglru_bwd, c_ctx, w_ada, b_ada, g_pre_mix, g_post_mix, g_pre_ffn, g_post_ffn, w_in, g_qa, g_ka, g_ckv, w_uk, w_uv, conv_rnn_w, conv_rnn_b, w_rg, b_rg, w_ig, b_ig, lam, w_oa, w_ob, w_oc, w_out, w_up, conv_ffn_w, conv_ffn_b, w_down):
    x = (x_prompt.reshape(N_CTX, D_MODEL), x_sample.reshape(N_LAT, D_MODEL))
    cvec = jnp.concatenate([c_ctx[None, :], c, jnp.zeros((N_MOD_ROWS - 1 - DEC_BATCH, D_MODEL), F32)], axis=0)
    mods = _modulation(cvec, w_ada, b_ada).reshape(DEPTH * N_MOD_ROWS, 1, 6 * D_MODEL)

    vec = lambda g: g.reshape(DEPTH, 1, -1)
    tab = _rope_tables()
    seg = np.arange(QA_W) // HD_A
    bd = jnp.asarray(np.where(seg[:, None] == seg[None, :], 1.0 / HD_A, 0.0), BF16)
    lane = np.arange(QB_PAD) % LANES
    place32_np = (lane[None, :] == KR_LANE + np.arange(ROPE_B)[:, None]).astype(np.float32)
    place32 = jnp.asarray(place32_np, BF16)
    place = jnp.asarray(np.pad(place32_np, ((KR_LANE, LANES - KR_LANE - ROPE_B), (0, 0))), BF16)
    w_att = w_xy = w_gl = _pack_w_proj(w_in)
    w_kv = _pack_w_kv(w_uk, w_uv)
    gqa_t = jnp.tile(g_qa, (1, H_A)).reshape(DEPTH, 1, QA_W)
    gka_t = jnp.tile(g_ka, (1, KV_A)).reshape(DEPTH, 1, KA_W)
    wg = (0.5 * jnp.concatenate([w_rg[:, 0], w_ig[:, 0], w_rg[:, 1], w_ig[:, 1]], axis=-1)).astype(BF16)
    bg = 0.5 * jnp.stack([b_rg[:, 0], b_ig[:, 0], b_rg[:, 1], b_ig[:, 1]], axis=1)
    h0_lat = jnp.stack([state_rglru_fwd, state_rglru_bwd], axis=0)
    h0_zero = jnp.zeros((BATCH, 2, D_RNN), F32)
    w_oa_b, w_ob_b, w_oc_b, w_out_b = (w.astype(BF16) for w in (w_oa, w_ob, w_oc, w_out))
    w_up_b, w_down_b = w_up.astype(BF16), w_down.astype(BF16)
    conv_ffn_b3 = vec(conv_ffn_b)

    kc_a = _cache_dup_heads(cache_gqa_k, False)
    vc_a = _cache_dup_heads(cache_gqa_v, True)
    kc_b, vc_b = _kvup_cache(cache_mla_ckv, cache_mla_krope, w_kv, place32)

    lat_blk0 = N_CTX // DEC_SEQ
    gqa_k = gqa_v = lambda j: j // 2
    mla_ke = lambda j: 2 * j
    mla_ko = lambda j: 2 * j + 1
    mla_v = lambda j: j

    def new_seg(k, v, rows, blk0, fe, fo, fv):
        at = lambda f: lambda b, j: (f(j), blk0 + b, 0)
        return (k, v, (None, rows, LANES), (None, rows, 2 * LANES), at(fe), at(fo), at(fv))

    new_k, new_v, new_ckv, new_kr, new_fwd, new_bwd = [], [], [], [], [], []
    for l in range(DEPTH):
        qa, qb, ka, va, kdup, vdup, ckv, kr, kb, vb, h = _inproj(
            l, *x, mods, vec(g_pre_mix), tab, w_att, bd, gqa_t, gka_t, vec(g_ckv), w_kv, place)

        def cache_seg(k, v, fe, fo, fv):
            at = lambda f: lambda b, j: (l, f(j), b, 0)
            return (k, v, (None, None, PAST_LEN, LANES), (None, None, PAST_LEN, 2 * LANES), at(fe), at(fo), at(fv))

        gqa_idx = (gqa_k, gqa_k, gqa_v)
        gqa = dict(qe_lane=lambda j: j, qo_lane=lambda j: j, mask_q=True)
        oa_ctx = _attention(qa, [new_seg(kdup, vdup, SEQ, 0, *gqa_idx)],
                            n_batch=BATCH, t_len=SEQ, tok0=0, pairs_per_step=N_PAIR, name="gqa_ctx", **gqa)
        oa_lat = _attention(qa, [cache_seg(kc_a, vc_a, *gqa_idx), new_seg(kdup, vdup, DEC_SEQ, lat_blk0, *gqa_idx)],
                            n_batch=DEC_BATCH, t_len=DEC_SEQ, tok0=N_CTX, pairs_per_step=2, name="gqa_lat", **gqa)
        mla_idx = (mla_ke, mla_ko, mla_v)
        mla = dict(qe_lane=mla_ke, qo_lane=mla_ko, mask_q=False)
        ob_ctx = _attention(qb, [new_seg(kb, vb, SEQ, 0, *mla_idx)],
                            n_batch=BATCH, t_len=SEQ, tok0=0, pairs_per_step=N_PAIR, name="mla_ctx", **mla)
        ob_lat = _attention(qb, [cache_seg(kc_b, vc_b, *mla_idx), new_seg(kb, vb, DEC_SEQ, lat_blk0, *mla_idx)],
                            n_batch=DEC_BATCH, t_len=DEC_SEQ, tok0=N_CTX, pairs_per_step=2, name="mla_lat", **mla)

        rnn_args = (w_xy, conv_rnn_w, vec(conv_rnn_b), wg, bg, lam)
        oc_ctx, st_ctx = _rglru(l, h, *rnn_args, h0_zero,
                                n_seq=BATCH, t_len=SEQ, tok0=0, n_sub=RNN_CTX_SUB, cb=RNN_CB, name="rglru_ctx")
        oc_lat, _ = _rglru(l, h, *rnn_args, jnp.moveaxis(h0_lat[:, :, l], 0, 1),
                           n_seq=DEC_BATCH, t_len=DEC_SEQ, tok0=N_CTX, n_sub=1, cb=2 * RNN_CB, name="rglru_lat")

        x1, h2 = _merge(l, (oa_ctx, oa_lat), (ob_ctx, ob_lat), (oc_ctx, oc_lat), x, h, mods,
                        vec(g_post_mix), vec(g_pre_ffn), w_oa_b, w_ob_b, w_oc_b, w_gl, w_out_b)
        ffn_args = (l, h2, x1, mods, vec(g_post_ffn), w_up_b, conv_ffn_w, conv_ffn_b3, w_down_b)
        x = (_ffn(*ffn_args, tok0=0, n_tok=N_CTX, t_len=SEQ, name="ffn_ctx"),
             _ffn(*ffn_args, tok0=N_CTX, n_tok=N_LAT, t_len=DEC_SEQ, name="ffn_lat"))

        new_k.append(ka[:N_CTX].reshape(BATCH, SEQ, KV_A, HD_A))
        new_v.append(va[:N_CTX].reshape(BATCH, SEQ, KV_A, HD_A))
        new_ckv.append(ckv[:N_CTX].reshape(BATCH, SEQ, KV_RANK))
        new_kr.append(kr[:N_CTX].reshape(BATCH, SEQ, ROPE_B))
        new_fwd.append(st_ctx[:, 0])
        new_bwd.append(st_ctx[:, 1])

    stack = lambda xs: jnp.stack(xs, axis=1)
    return (x[0].reshape(BATCH, SEQ, D_MODEL), x[1].reshape(DEC_BATCH, DEC_SEQ, D_MODEL),
            stack(new_k), stack(new_v), stack(new_ckv), stack(new_kr), stack(new_fwd), stack(new_bwd))
```
